```python
import math, functools
import jax, jax.numpy as jnp
from jax import lax
import numpy as np

D_MODEL = 1024
BATCH = 2
SEQ = 8192
DEPTH = 4
DEC_BATCH = 32
DEC_SEQ = 4
PAST_LEN = 8192
PAGE_SIZE = 128

SSD_INNER = 2 * D_MODEL
SSD_HEADDIM = 64
SSD_HEADS = SSD_INNER // SSD_HEADDIM
SSD_GROUPS = 4
SSD_HPG = SSD_HEADS // SSD_GROUPS
D_STATE = 128
CONV_W = 4
CONV_DIM = SSD_INNER + 2 * SSD_GROUPS * D_STATE
SSD_CHUNK = 128
DA_HEADS = 8
DA_KV_HEADS = 4
DA_GROUP = DA_HEADS // DA_KV_HEADS
DA_DH = 64
DA_DV = 2 * DA_DH
ROPE_DIM = DA_DH // 4
ROPE_THETA = 500000.0
Q_BLOCK = 128
N_MEM = 256
CA_HEADS = 4
CA_DH = D_MODEL // CA_HEADS
D_FF = 4 * D_MODEL
IN_SPLITS = (SSD_INNER, CONV_DIM, SSD_HEADS, DA_HEADS * 2 * DA_DH,
             DA_KV_HEADS * 2 * DA_DH, DA_KV_HEADS * DA_DV, 2 * D_MODEL)
N_IN = sum(IN_SPLITS)
SPLIT_POINTS = tuple(int(v) for v in np.cumsum(IN_SPLITS)[:-1])
DN_ALPHA = (2 * DEPTH) ** 0.25
DN_BETA = (8 * DEPTH) ** -0.25
LN_EPS = 1e-5
RMS_EPS = 1e-6

kernel_name = 'hybrid_ssd_diffattn_decoder_step'


def layer_norm(x, g, b):
    xf = x.astype(jnp.float32)
    mu = jnp.mean(xf, -1, keepdims=True)
    var = jnp.mean(jnp.square(xf - mu), -1, keepdims=True)
    return ((xf - mu) * lax.rsqrt(var + LN_EPS) * g + b).astype(x.dtype)


def rms_norm(x, w):
    xf = x.astype(jnp.float32)
    y = xf * lax.rsqrt(jnp.mean(jnp.square(xf), -1, keepdims=True) + RMS_EPS)
    return (y * w).astype(x.dtype)


def rope_partial(x, pos):
    half = ROPE_DIM // 2
    inv_freq = ROPE_THETA ** (-jnp.arange(half, dtype=jnp.float32) / half)
    ang = pos.astype(jnp.float32)[:, None] * inv_freq[None, :]
    bshape = (pos.shape[0],) + (1,) * (x.ndim - 3) + (half,)
    cos = jnp.cos(ang).reshape(bshape)
    sin = jnp.sin(ang).reshape(bshape)
    x1 = x[..., :half].astype(jnp.float32)
    x2 = x[..., half:ROPE_DIM].astype(jnp.float32)
    rot = jnp.concatenate([x1 * cos - x2 * sin, x2 * cos + x1 * sin], -1).astype(x.dtype)
    return jnp.concatenate([rot, x[..., ROPE_DIM:]], -1)


def causal_dwconv(xpad, w, b):
    out = lax.conv_general_dilated(xpad, w[:, None, :].astype(xpad.dtype), window_strides=(1,),
                                   padding='VALID', dimension_numbers=('NWC', 'WIO', 'NWC'),
                                   feature_group_count=xpad.shape[-1])
    return out + b


def ssd_scan(xh, dt, A, Bm, Cm, h0):
    f32 = jnp.float32
    Bsz, L, G, J, P = xh.shape
    N = Bm.shape[-1]
    Q = SSD_CHUNK if L % SSD_CHUNK == 0 else L
    nc = L // Q
    a = (dt * A).reshape(Bsz, nc, Q, G, J)
    xdt = (xh.astype(f32) * dt[..., None]).reshape(Bsz, nc, Q, G, J, P)
    Bc = Bm.astype(f32).reshape(Bsz, nc, Q, G, N)
    Cc = Cm.astype(f32).reshape(Bsz, nc, Q, G, N)
    acum = jnp.cumsum(a, axis=2)
    seg = acum[:, :, :, None] - acum[:, :, None]
    causal = jnp.tril(jnp.ones((Q, Q), bool))[None, None, :, :, None, None]
    decay = jnp.exp(jnp.where(causal, seg, -jnp.inf))
    cb = jnp.einsum('bclgn,bcsgn->bclsg', Cc, Bc)
    y_diag = jnp.einsum('bclsg,bclsgj,bcsgjp->bclgjp', cb, decay, xdt)
    decay_end = jnp.exp(acum[:, :, -1:] - acum)
    states = jnp.einsum('bclgn,bclgj,bclgjp->bcgjpn', Bc, decay_end, xdt)
    chunk_decay = jnp.exp(acum[:, :, -1])

    def step(h, inp):
        s, d = inp
        return h * d[..., None, None] + s, h

    h_final, h_prev = lax.scan(step, h0.astype(f32),
                               (jnp.moveaxis(states, 1, 0), jnp.moveaxis(chunk_decay, 1, 0)))
    h_prev = jnp.moveaxis(h_prev, 0, 1)
    y_off = jnp.einsum('bclgn,bcgjpn,bclgj->bclgjp', Cc, h_prev, jnp.exp(acum))
    y = (y_diag + y_off).reshape(Bsz, L, G, J, P)
    return y.astype(xh.dtype), h_final.astype(h0.dtype)


def ssd_branch(z, xbc, dt_raw, conv_prev, h0, conv_w, conv_b, dt_bias, a_log, d_skip, norm_w, w_br):
    Bsz, L, _ = z.shape
    xpad = jnp.concatenate([conv_prev.astype(xbc.dtype), xbc], axis=1)
    new_conv = xpad[:, -(CONV_W - 1):]
    xbc = jax.nn.silu(causal_dwconv(xpad, conv_w, conv_b))
    xs, Bm, Cm = jnp.split(xbc, [SSD_INNER, SSD_INNER + SSD_GROUPS * D_STATE], axis=-1)
    xh = xs.reshape(Bsz, L, SSD_GROUPS, SSD_HPG, SSD_HEADDIM)
    dt = jax.nn.softplus(dt_raw.astype(jnp.float32) + dt_bias.astype(jnp.float32))
    dt = dt.reshape(Bsz, L, SSD_GROUPS, SSD_HPG)
    A = -jnp.exp(a_log.astype(jnp.float32)).reshape(SSD_GROUPS, SSD_HPG)
    y, h = ssd_scan(xh, dt, A, Bm.reshape(Bsz, L, SSD_GROUPS, D_STATE),
                    Cm.reshape(Bsz, L, SSD_GROUPS, D_STATE),
                    h0.reshape(h0.shape[0], SSD_GROUPS, SSD_HPG, SSD_HEADDIM, D_STATE))
    y = y + xh * d_skip.reshape(SSD_GROUPS, SSD_HPG)[..., None]
    y = rms_norm(y.reshape(Bsz, L, SSD_INNER) * jax.nn.silu(z), norm_w)
    return y @ w_br, new_conv, h.reshape(h.shape[0], SSD_HEADS, SSD_HEADDIM, D_STATE)


def diff_attention(q, k_all, v_all, mask, lam, lam_init, subln_w):
    B, T = q.shape[:2]
    s = jnp.einsum('btkgid,bskid->bkgits', q, k_all).astype(jnp.float32) * (DA_DH ** -0.5)
    p = jax.nn.softmax(jnp.where(mask, s, -jnp.inf), axis=-1)
    w = p[:, :, :, 0] - lam * p[:, :, :, 1]
    o = jnp.einsum('bkgts,bskv->btkgv', w.astype(v_all.dtype), v_all)
    o = rms_norm(o, subln_w) * (1.0 - lam_init)
    return o.reshape(B, T, DA_HEADS * DA_DV)


def prompt_diff_attention(q, k, v, lam, lam_init, subln_w):
    B, S = q.shape[:2]
    nb = S // Q_BLOCK
    qb = jnp.moveaxis(q.reshape((B, nb, Q_BLOCK) + q.shape[2:]), 1, 0)
    kpos = jnp.arange(S)

    def block(args):
        qi, i = args
        qpos = i * Q_BLOCK + jnp.arange(Q_BLOCK)
        return diff_attention(qi, k, v, qpos[:, None] >= kpos[None, :], lam, lam_init, subln_w)

    ob = lax.map(block, (qb, jnp.arange(nb)))
    return jnp.moveaxis(ob, 0, 1).reshape(B, S, DA_HEADS * DA_DV)


def sample_diff_attention(q, k, v, lam, lam_init, subln_w, k_past, v_past):
    T = q.shape[1]
    P = k_past.shape[1]
    k_all = jnp.concatenate([k_past.astype(k.dtype), k], axis=1)
    v_all = jnp.concatenate([v_past.astype(v.dtype), v], axis=1)
    mask = (P + jnp.arange(T))[:, None] >= jnp.arange(P + T)[None, :]
    return diff_attention(q, k_all, v_all, mask, lam, lam_init, subln_w)


def hybrid_mixer(x, pos, attend, conv_prev, h0, lam, lam_init, lp):
    B, T, _ = x.shape
    z, xbc, dt_raw, q, k, v, g = jnp.split(x @ lp['w_in'], SPLIT_POINTS, axis=-1)
    ssd_out, new_conv, new_h = ssd_branch(z, xbc, dt_raw, conv_prev, h0, lp['conv_w'], lp['conv_b'],
                                          lp['dt_bias'], lp['a_log'], lp['d_skip'],
                                          lp['ssd_norm_w'], lp['w_ssd_br'])
    q = rope_partial(q.reshape(B, T, DA_KV_HEADS, DA_GROUP, 2, DA_DH), pos)
    k = rope_partial(k.reshape(B, T, DA_KV_HEADS, 2, DA_DH), pos)
    v = v.reshape(B, T, DA_KV_HEADS, DA_DV)
    att_out = attend(q, k, v, lam, lam_init, lp['subln_w']) @ lp['w_att_br']
    g_ssd, g_att = jnp.split(jax.nn.sigmoid(g), 2, axis=-1)
    merged = g_ssd * ssd_out + g_att * att_out
    return merged @ lp['w_o'], k, v, new_conv, new_h


def cross_attention(x, mem_k, mem_v, w_q, w_o):
    B, T, _ = x.shape
    q = (x @ w_q).reshape(B, T, CA_HEADS, CA_DH)
    s = jnp.einsum('bthd,bmhd->bhtm', q, mem_k).astype(jnp.float32) * (CA_DH ** -0.5)
    p = jax.nn.softmax(s, axis=-1).astype(mem_v.dtype)
    o = jnp.einsum('bhtm,bmhd->bthd', p, mem_v).reshape(B, T, CA_HEADS * CA_DH)
    return o @ w_o


def sq_relu_mlp(x, w_up, w_down):
    return jnp.square(jax.nn.relu(x @ w_up)) @ w_down


def trunk_layer(x, pos, attend, conv_prev, h0, mem_k, mem_v, lam, lam_init, lp):
    mix, k_new, v_new, conv_new, h_new = hybrid_mixer(x, pos, attend, conv_prev, h0, lam, lam_init, lp)
    x = layer_norm(DN_ALPHA * x + mix, lp['ln1_g'], lp['ln1_b'])
    x = layer_norm(DN_ALPHA * x + cross_attention(x, mem_k, mem_v, lp['w_cq'], lp['w_co']),
                   lp['ln2_g'], lp['ln2_b'])
    x = layer_norm(DN_ALPHA * x + sq_relu_mlp(x, lp['w_up'], lp['w_down']), lp['ln3_g'], lp['ln3_b'])
    return x, k_new, v_new, conv_new, h_new


def setup_inputs(seed: int = 0) -> dict:
    key = jax.random.key(seed)
    ks = iter(jax.random.split(key, 64))
    f32 = jnp.float32

    def nrm(shape, scale):
        return scale * jax.random.normal(next(ks), shape, f32)

    def gain(shape):
        return 1.0 + nrm(shape, 0.02)

    n_pages = PAST_LEN // PAGE_SIZE
    n_used = DEC_BATCH * n_pages
    n_pool = n_used + max(1, n_used // 4)
    page_table = jax.random.permutation(next(ks), n_pool)[:n_used].reshape(DEC_BATCH, n_pages).astype(jnp.int32)
    dt0 = jnp.exp(jax.random.uniform(next(ks), (DEPTH, SSD_HEADS), f32, math.log(1e-3), math.log(1e-1)))
    dt_bias = dt0 + jnp.log(-jnp.expm1(-dt0))
    a_log = jnp.log(jax.random.uniform(next(ks), (DEPTH, SSD_HEADS), f32, 1.0, 16.0))
    return {
        'x_prompt': nrm((BATCH, SEQ, D_MODEL), 1.0),
        'x_sample': nrm((DEC_BATCH, DEC_SEQ, D_MODEL), 1.0),
        'mem_prompt': nrm((BATCH, N_MEM, D_MODEL), 1.0),
        'cache_k': nrm((DEPTH, n_pool, PAGE_SIZE, DA_KV_HEADS, 2, DA_DH), 1.0),
        'cache_v': nrm((DEPTH, n_pool, PAGE_SIZE, DA_KV_HEADS, DA_DV), 1.0),
        'state_ssm': nrm((DEPTH, DEC_BATCH, SSD_HEADS, SSD_HEADDIM, D_STATE), 0.1),
        'state_conv': nrm((DEPTH, DEC_BATCH, CONV_W - 1, CONV_DIM), 1.0),
        'cache_mem_k': nrm((DEPTH, DEC_BATCH, N_MEM, CA_HEADS, CA_DH), 1.0),
        'cache_mem_v': nrm((DEPTH, DEC_BATCH, N_MEM, CA_HEADS, CA_DH), 1.0),
        'page_table': page_table,
        'w_in': nrm((DEPTH, D_MODEL, N_IN), D_MODEL ** -0.5),
        'conv_w': nrm((DEPTH, CONV_W, CONV_DIM), CONV_W ** -0.5),
        'conv_b': nrm((DEPTH, CONV_DIM), 0.01),
        'dt_bias': dt_bias,
        'a_log': a_log,
        'd_skip': gain((DEPTH, SSD_HEADS)),
        'ssd_norm_w': gain((DEPTH, SSD_INNER)),
        'w_ssd_br': nrm((DEPTH, SSD_INNER, D_MODEL), SSD_INNER ** -0.5),
        'lambda_q': nrm((DEPTH, 2, DA_DH), 0.1),
        'lambda_k': nrm((DEPTH, 2, DA_DH), 0.1),
        'subln_w': gain((DEPTH, DA_DV)),
        'w_att_br': nrm((DEPTH, DA_HEADS * DA_DV, D_MODEL), (DA_HEADS * DA_DV) ** -0.5),
        'w_o': nrm((DEPTH, D_MODEL, D_MODEL), DN_BETA * D_MODEL ** -0.5),
        'ln1_g': gain((DEPTH, D_MODEL)),
        'ln1_b': nrm((DEPTH, D_MODEL), 0.01),
        'w_cq': nrm((DEPTH, D_MODEL, CA_HEADS * CA_DH), D_MODEL ** -0.5),
        'w_ck': nrm((DEPTH, D_MODEL, CA_HEADS * CA_DH), D_MODEL ** -0.5),
        'w_cv': nrm((DEPTH, D_MODEL, CA_HEADS * CA_DH), DN_BETA * D_MODEL ** -0.5),
        'w_co': nrm((DEPTH, CA_HEADS * CA_DH, D_MODEL), DN_BETA * (CA_HEADS * CA_DH) ** -0.5),
        'ln2_g': gain((DEPTH, D_MODEL)),
        'ln2_b': nrm((DEPTH, D_MODEL), 0.01),
        'w_up': nrm((DEPTH, D_MODEL, D_FF), D_MODEL ** -0.5),
        'w_down': nrm((DEPTH, D_FF, D_MODEL), DN_BETA * D_FF ** -0.5),
        'ln3_g': gain((DEPTH, D_MODEL)),
        'ln3_b': nrm((DEPTH, D_MODEL), 0.01),
    }


def reference(x_prompt, x_sample, mem_prompt, cache_k, cache_v, state_ssm, state_conv,
              cache_mem_k, cache_mem_v, page_table, w_in, conv_w, conv_b, dt_bias, a_log,
              d_skip, ssd_norm_w, w_ssd_br, lambda_q, lambda_k, subln_w, w_att_br, w_o,
              ln1_g, ln1_b, w_cq, w_ck, w_cv, w_co, ln2_g, ln2_b, w_up, w_down, ln3_g, ln3_b):
    f32 = jnp.float32
    n_prompt = x_prompt.shape[0]
    n_dec = x_sample.shape[0]
    pos_p = jnp.arange(x_prompt.shape[1], dtype=jnp.int32)
    pos_s = PAST_LEN + jnp.arange(x_sample.shape[1], dtype=jnp.int32)
    xp, xs = x_prompt, x_sample
    kp_l, vp_l, hp_l, cp_l, mkp_l, mvp_l = [], [], [], [], [], []
    ks_l, vs_l, hs_l, cs_l = [], [], [], []
    for l in range(DEPTH):
        lp = {
            'w_in': w_in[l], 'conv_w': conv_w[l], 'conv_b': conv_b[l], 'dt_bias': dt_bias[l],
            'a_log': a_log[l], 'd_skip': d_skip[l], 'ssd_norm_w': ssd_norm_w[l],
            'w_ssd_br': w_ssd_br[l], 'subln_w': subln_w[l], 'w_att_br': w_att_br[l], 'w_o': w_o[l],
            'ln1_g': ln1_g[l], 'ln1_b': ln1_b[l], 'w_cq': w_cq[l], 'w_co': w_co[l],
            'ln2_g': ln2_g[l], 'ln2_b': ln2_b[l], 'w_up': w_up[l], 'w_down': w_down[l],
            'ln3_g': ln3_g[l], 'ln3_b': ln3_b[l],
        }
        lam_init = 0.8 - 0.6 * math.exp(-0.3 * l)
        lq = lambda_q[l].astype(f32)
        lk = lambda_k[l].astype(f32)
        lam = jnp.exp(jnp.sum(lq[0] * lk[0])) - jnp.exp(jnp.sum(lq[1] * lk[1])) + lam_init

        mk_p = (mem_prompt @ w_ck[l]).reshape(n_prompt, N_MEM, CA_HEADS, CA_DH)
        mv_p = (mem_prompt @ w_cv[l]).reshape(n_prompt, N_MEM, CA_HEADS, CA_DH)
        conv0 = jnp.zeros((n_prompt, CONV_W - 1, CONV_DIM), xp.dtype)
        h0 = jnp.zeros((n_prompt, SSD_HEADS, SSD_HEADDIM, D_STATE), f32)
        xp, k_p, v_p, c_p, h_p = trunk_layer(xp, pos_p, prompt_diff_attention, conv0, h0,
                                             mk_p, mv_p, lam, lam_init, lp)

        k_past = cache_k[l][page_table].reshape(n_dec, -1, DA_KV_HEADS, 2, DA_DH)
        v_past = cache_v[l][page_table].reshape(n_dec, -1, DA_KV_HEADS, DA_DV)
        attend_s = functools.partial(sample_diff_attention, k_past=k_past, v_past=v_past)
        xs, k_s, v_s, c_s, h_s = trunk_layer(xs, pos_s, attend_s, state_conv[l], state_ssm[l],
                                             cache_mem_k[l], cache_mem_v[l], lam, lam_init, lp)

        kp_l.append(k_p); vp_l.append(v_p); hp_l.append(h_p); cp_l.append(c_p)
        mkp_l.append(mk_p); mvp_l.append(mv_p)
        ks_l.append(k_s); vs_l.append(v_s); hs_l.append(h_s); cs_l.append(c_s)

    return (xp, xs,
            jnp.stack(kp_l), jnp.stack(vp_l), jnp.stack(hp_l), jnp.stack(cp_l),
            jnp.stack(mkp_l), jnp.stack(mvp_l),
            jnp.stack(ks_l), jnp.stack(vs_l), jnp.stack(hs_l), jnp.stack(cs_l))
```

```python
import functools
import math

import jax
import jax.numpy as jnp
import numpy as np
from jax import lax
from jax.experimental import pallas as pl
from jax.experimental.pallas import tpu as pltpu

F32 = jnp.float32
BF16 = jnp.bfloat16

D_MODEL = 1024
DEPTH = 4
PAST_LEN = 8192
PAGE_SIZE = 128
SSD_INNER = 2 * D_MODEL
SSD_HEADDIM = 64
SSD_HEADS = SSD_INNER // SSD_HEADDIM
SSD_GROUPS = 4
SSD_HPG = SSD_HEADS // SSD_GROUPS
D_STATE = 128
CONV_W = 4
CONV_DIM = SSD_INNER + 2 * SSD_GROUPS * D_STATE
SSD_CHUNK = 128
DA_HEADS = 8
DA_KV_HEADS = 4
DA_GROUP = DA_HEADS // DA_KV_HEADS
DA_DH = 64
DA_DV = 2 * DA_DH
ROPE_DIM = DA_DH // 4
ROPE_THETA = 500000.0
N_MEM = 256
CA_HEADS = 4
CA_DH = D_MODEL // CA_HEADS
D_FF = 4 * D_MODEL
IN_SPLITS = (SSD_INNER, CONV_DIM, SSD_HEADS, DA_HEADS * 2 * DA_DH,
             DA_KV_HEADS * 2 * DA_DH, DA_KV_HEADS * DA_DV, 2 * D_MODEL)
DN_ALPHA = (2 * DEPTH) ** 0.25
LN_EPS = 1e-5
RMS_EPS = 1e-6

LANES = 128
NEG_BIG = -1e30
VMEM_LIMIT = 56 * 1024 * 1024
PAGES_PER_STEP = 4
NEW_ROWS = 16

NT_DIMS = (((1,), (1,)), ((), ()))
TN_DIMS = (((0,), (0,)), ((), ()))


def _params(sem):
    return pltpu.CompilerParams(dimension_semantics=sem, vmem_limit_bytes=VMEM_LIMIT)


def _dot(a, b):
    return jnp.dot(a, b, preferred_element_type=F32)


def _dot_nt(a, b):
    return lax.dot_general(a, b, NT_DIMS, preferred_element_type=F32)


def _dot_tn(a, b):
    return lax.dot_general(a, b, TN_DIMS, preferred_element_type=F32)


def _split3(a):
    hi = a.astype(BF16)
    r = a - hi.astype(F32)
    mid = r.astype(BF16)
    lo = (r - mid.astype(F32)).astype(BF16)
    return hi, mid, lo


def _sigmoid(x):
    return 1.0 / (1.0 + jnp.exp(-x))


def _layer_norm(v, g, b):
    mu = jnp.mean(v, axis=-1, keepdims=True)
    d = v - mu
    var = jnp.mean(d * d, axis=-1, keepdims=True)
    return d * lax.rsqrt(var + LN_EPS) * g + b


def _mm_kernel(x_ref, w_ref, o_ref):
    o_ref[...] = _dot(x_ref[...].astype(BF16), w_ref[...]).astype(o_ref.dtype)


def _mm_rope_kernel(x_ref, w_ref, c_ref, s1_ref, s2_ref, o_ref):
    acc = _dot(x_ref[...].astype(BF16), w_ref[...])
    c, s1, s2 = c_ref[...], s1_ref[...], s2_ref[...]
    for cb in range(acc.shape[1] // LANES):
        blk = acc[:, cb * LANES:(cb + 1) * LANES]
        up = pltpu.roll(blk, LANES - ROPE_DIM // 2, 1)
        dn = pltpu.roll(blk, ROPE_DIM // 2, 1)
        o_ref[:, cb * LANES:(cb + 1) * LANES] = (blk * c + up * s1 + dn * s2).astype(o_ref.dtype)


def _mm(x, w, out_dtype, bm, bn, rope=None):
    m, k = x.shape
    n = w.shape[1]
    bm, bn = min(bm, m), min(bn, n)
    if rope is not None:
        bm = min(bm, rope[0].shape[0])
    in_specs = [pl.BlockSpec((bm, k), lambda i, j: (i, 0)),
                pl.BlockSpec((k, bn), lambda i, j: (0, j))]
    args = [x, w]
    body = _mm_kernel
    if rope is not None:
        nrep = rope[0].shape[0] // bm
        for t in rope:
            in_specs.append(pl.BlockSpec((bm, LANES), lambda i, j: (i % nrep, 0)))
            args.append(t)
        body = _mm_rope_kernel
    return pl.pallas_call(
        body, grid=(m // bm, n // bn), in_specs=in_specs,
        out_specs=pl.BlockSpec((bm, bn), lambda i, j: (i, j)),
        out_shape=jax.ShapeDtypeStruct((m, n), out_dtype),
        compiler_params=_params(("parallel", "parallel")),
    )(*args)


def _mm_res_ln_kernel(a_ref, w_ref, x_ref, g_ref, b_ref, o_ref):
    y = _dot(a_ref[...], w_ref[...])
    o_ref[...] = _layer_norm(DN_ALPHA * x_ref[...] + y, g_ref[...], b_ref[...])


def _mm_res_ln(a, w, x, g, b, bm):
    m, k = a.shape
    n = w.shape[1]
    bm = min(bm, m)
    return pl.pallas_call(
        _mm_res_ln_kernel, grid=(m // bm,),
        in_specs=[pl.BlockSpec((bm, k), lambda i: (i, 0)),
                  pl.BlockSpec((k, n), lambda i: (0, 0)),
                  pl.BlockSpec((bm, n), lambda i: (i, 0)),
                  pl.BlockSpec((1, n), lambda i: (0, 0)),
                  pl.BlockSpec((1, n), lambda i: (0, 0))],
        out_specs=pl.BlockSpec((bm, n), lambda i: (i, 0)),
        out_shape=jax.ShapeDtypeStruct((m, n), F32),
        compiler_params=_params(("parallel",)),
    )(a, w, x, g, b)


def _merge_kernel(y_ref, o_ref, g_ref, x_ref, wbr_ref, watt_ref, wo_ref, lg_ref, lb_ref, out_ref):
    ssd = _dot(y_ref[...], wbr_ref[...])
    att = _dot(o_ref[...], watt_ref[...])
    g = g_ref[...].astype(F32)
    merged = _sigmoid(g[:, :D_MODEL]) * ssd + _sigmoid(g[:, D_MODEL:]) * att
    mix = _dot(merged.astype(BF16), wo_ref[...])
    out_ref[...] = _layer_norm(DN_ALPHA * x_ref[...] + mix, lg_ref[...], lb_ref[...])


def _merge(y, o, g, x, wbr, watt, wo, lg, lb, bm):
    m = x.shape[0]
    bm = min(bm, m)
    row = lambda w: pl.BlockSpec((bm, w), lambda i: (i, 0))
    full = lambda a: pl.BlockSpec(a.shape, lambda i: (0, 0))
    return pl.pallas_call(
        _merge_kernel, grid=(m // bm,),
        in_specs=[row(SSD_INNER), row(D_MODEL), row(2 * D_MODEL), row(D_MODEL),
                  full(wbr), full(watt), full(wo), full(lg), full(lb)],
        out_specs=row(D_MODEL),
        out_shape=jax.ShapeDtypeStruct((m, D_MODEL), F32),
        compiler_params=_params(("parallel",)),
    )(y, o, g, x, wbr, watt, wo, lg, lb)


def _mlp_kernel(x_ref, wu_ref, wd_ref, lg_ref, lb_ref, o_ref, acc_sc, xb_sc):
    j = pl.program_id(1)

    @pl.when(j == 0)
    def _():
        acc_sc[...] = jnp.zeros_like(acc_sc)
        xb_sc[...] = x_ref[...].astype(BF16)

    h = jnp.maximum(_dot(xb_sc[...], wu_ref[...]), 0.0)
    acc_sc[...] += _dot((h * h).astype(BF16), wd_ref[...])

    @pl.when(j == pl.num_programs(1) - 1)
    def _():
        o_ref[...] = _layer_norm(DN_ALPHA * x_ref[...] + acc_sc[...], lg_ref[...], lb_ref[...])


def _mlp(x, wu, wd, lg, lb, bm, bf):
    m = x.shape[0]
    bm = min(bm, m)
    return pl.pallas_call(
        _mlp_kernel, grid=(m // bm, D_FF // bf),
        in_specs=[pl.BlockSpec((bm, D_MODEL), lambda i, j: (i, 0)),
                  pl.BlockSpec((D_MODEL, bf), lambda i, j: (0, j)),
                  pl.BlockSpec((bf, D_MODEL), lambda i, j: (j, 0)),
                  pl.BlockSpec((1, D_MODEL), lambda i, j: (0, 0)),
                  pl.BlockSpec((1, D_MODEL), lambda i, j: (0, 0))],
        out_specs=pl.BlockSpec((bm, D_MODEL), lambda i, j: (i, 0)),
        out_shape=jax.ShapeDtypeStruct((m, D_MODEL), F32),
        scratch_shapes=[pltpu.VMEM((bm, D_MODEL), F32), pltpu.VMEM((bm, D_MODEL), BF16)],
        compiler_params=_params(("parallel", "arbitrary")),
    )(x, wu, wd, lg, lb)


def _cross_kernel(q_ref, k_ref, v_ref, o_ref):
    q = q_ref[0]
    k = k_ref[0]
    v = v_ref[0]
    for h in range(CA_HEADS):
        sl = slice(h * CA_DH, (h + 1) * CA_DH)
        s = _dot_nt(q[:, sl], k[:, sl]) * (CA_DH ** -0.5)
        s = s - jnp.max(s, axis=-1, keepdims=True)
        p = jnp.exp(s)
        p = p / jnp.sum(p, axis=-1, keepdims=True)
        o_ref[0, :, sl] = _dot(p.astype(BF16), v[:, sl]).astype(o_ref.dtype)


def _cross(q, mk, mv, tq):
    b, t, d = q.shape
    tq = min(tq, t)
    return pl.pallas_call(
        _cross_kernel, grid=(b, t // tq),
        in_specs=[pl.BlockSpec((1, tq, d), lambda i, j: (i, j, 0)),
                  pl.BlockSpec((1, N_MEM, d), lambda i, j: (i, 0, 0)),
                  pl.BlockSpec((1, N_MEM, d), lambda i, j: (i, 0, 0))],
        out_specs=pl.BlockSpec((1, tq, d), lambda i, j: (i, j, 0)),
        out_shape=jax.ShapeDtypeStruct((b, t, d), BF16),
        compiler_params=_params(("parallel", "parallel")),
    )(q, mk, mv)


def _ssd_kernel(xbc_ref, dt_ref, z_ref, h0_ref, cprev_ref, cw_ref, cb_ref, dtb_ref, alog_ref,
                dskip_ref, nw_ref, tri_ref, eye_ref, exp_ref, y_ref, hout_ref,
                h_sc, xpad_sc, y_sc, *, q_len, valid):
    c = pl.program_id(1)

    @pl.when(c == 0)
    def _():
        h_sc[...] = h0_ref[0]
        xpad_sc[0:8, :] = cprev_ref[0]

    xpad_sc[8:8 + q_len, :] = xbc_ref[0].astype(F32)
    cw = cw_ref[...]
    conv = cb_ref[...] + cw[3:4, :] * xpad_sc[8:8 + q_len, :]
    for kk in range(1, CONV_W):
        conv = conv + cw[3 - kk:4 - kk, :] * xpad_sc[8 - kk:8 - kk + q_len, :]
    xpad_sc[0:8, :] = xpad_sc[q_len:q_len + 8, :]
    xbc = conv * _sigmoid(conv)
    xs = xbc[:, :SSD_INNER]

    dtr = dt_ref[0] + dtb_ref[...]
    dtv = jnp.maximum(dtr, 0.0) + jnp.log(1.0 + jnp.exp(-jnp.abs(dtr)))
    if valid < q_len:
        rows = lax.broadcasted_iota(jnp.int32, dtv.shape, 0)
        dtv = jnp.where(rows < valid, dtv, 0.0)
    a = dtv * (-jnp.exp(alog_ref[...]))

    tri = tri_ref[...]
    eye = eye_ref[...]
    expand = exp_ref[...]
    a_parts = _split3(a)
    acum = sum(_dot(tri, p) for p in a_parts)
    a_t = sum(_dot_nt(eye, p) for p in a_parts)
    acum_t = sum(_dot_nt(p, tri) for p in _split3(a_t))
    acum_x = sum(_dot(p, expand) for p in _split3(acum))
    dt_x = sum(_dot(p, expand) for p in _split3(dtv))

    xdt = xs * dt_x
    xdt_b = xdt.astype(BF16)
    xdec = (xdt * jnp.exp(acum_x[q_len - 1:q_len, :] - acum_x)).astype(BF16)
    eacum_x = jnp.exp(acum_x)
    chunk_decay = jnp.exp(jnp.broadcast_to(acum_t[:, q_len - 1:q_len], (LANES, LANES)))

    li = lax.broadcasted_iota(jnp.int32, (q_len, q_len), 0)
    si = lax.broadcasted_iota(jnp.int32, (q_len, q_len), 1)
    causal = li >= si
    left = lax.broadcasted_iota(jnp.int32, (q_len, LANES), 1) < SSD_HEADDIM
    gw = SSD_HPG * SSD_HEADDIM
    for g in range(SSD_GROUPS):
        b_g = xbc[:, SSD_INNER + g * D_STATE:SSD_INNER + (g + 1) * D_STATE].astype(BF16)
        c_off = SSD_INNER + SSD_GROUPS * D_STATE
        c_g = xbc[:, c_off + g * D_STATE:c_off + (g + 1) * D_STATE].astype(BF16)
        cbm = _dot_nt(c_g, b_g)
        h_g = h_sc[g * gw:(g + 1) * gw, :]
        y_off = _dot_nt(c_g, h_g.astype(BF16))
        for pp in range(SSD_HPG // 2):
            j0 = g * SSD_HPG + 2 * pp
            cols = slice(j0 * SSD_HEADDIM, (j0 + 2) * SSD_HEADDIM)
            x_pair = xdt_b[:, cols]
            halves = []
            for j in (j0, j0 + 1):
                seg = acum[:, j:j + 1] - acum_t[j:j + 1, :]
                m_j = (cbm * jnp.where(causal, jnp.exp(seg), 0.0)).astype(BF16)
                halves.append(_dot(m_j, x_pair))
            y_diag = jnp.where(left, halves[0], halves[1])
            y_sc[:, cols] = (y_diag + y_off[:, pp * LANES:(pp + 1) * LANES] * eacum_x[:, cols]
                             + xs[:, cols] * dskip_ref[:, cols])
        new_states = _dot_tn(xdec[:, g * gw:(g + 1) * gw], b_g)
        for jj in range(SSD_HPG):
            j = g * SSD_HPG + jj
            r = slice(j * SSD_HEADDIM, (j + 1) * SSD_HEADDIM)
            rl = slice(jj * SSD_HEADDIM, (jj + 1) * SSD_HEADDIM)
            h_sc[r, :] = h_g[rl, :] * chunk_decay[j:j + 1, :] + new_states[rl, :]

    z = z_ref[0].astype(F32)
    yz = y_sc[...] * (z * _sigmoid(z))
    ms = jnp.mean(yz * yz, axis=-1, keepdims=True)
    y_ref[0] = (yz * lax.rsqrt(ms + RMS_EPS) * nw_ref[...]).astype(y_ref.dtype)

    @pl.when(c == pl.num_programs(1) - 1)
    def _():
        hout_ref[0] = h_sc[...]


def _ssd(xbc, dtr, z, h0, cprev, cw, cb, dtb, alog, dskip_x, nw, consts, valid):
    b, l, _ = xbc.shape
    q_len = SSD_CHUNK
    nc = l // q_len
    tri, eye, expand = consts
    tok = lambda w: pl.BlockSpec((1, q_len, w), lambda i, c: (i, c, 0))
    per_b = lambda r, w: pl.BlockSpec((1, r, w), lambda i, c: (i, 0, 0))
    full = lambda a: pl.BlockSpec(a.shape, lambda i, c: (0, 0))
    return pl.pallas_call(
        functools.partial(_ssd_kernel, q_len=q_len, valid=valid),
        grid=(b, nc),
        in_specs=[tok(CONV_DIM), tok(LANES), tok(SSD_INNER),
                  per_b(SSD_INNER, D_STATE), per_b(8, CONV_DIM),
                  full(cw), full(cb), full(dtb), full(alog), full(dskip_x), full(nw),
                  full(tri), full(eye), full(expand)],
        out_specs=[tok(SSD_INNER), per_b(SSD_INNER, D_STATE)],
        out_shape=[jax.ShapeDtypeStruct((b, l, SSD_INNER), BF16),
                   jax.ShapeDtypeStruct((b, SSD_INNER, D_STATE), F32)],
        scratch_shapes=[pltpu.VMEM((SSD_INNER, D_STATE), F32),
                        pltpu.VMEM((q_len + 8, CONV_DIM), F32),
                        pltpu.VMEM((q_len, SSD_INNER), F32)],
        compiler_params=_params(("parallel", "arbitrary")),
    )(xbc, dtr, z, h0, cprev, cw, cb, dtb, alog, dskip_x, nw, tri, eye, expand)


def _flash_kernel(lam_ref, q_ref, k_ref, v_ref, sw_ref, o_ref, q_sc, m_sc, l_sc, acc_sc, *, bq, bk):
    qi = pl.program_id(2)
    kj = pl.program_id(3)
    rows2 = DA_GROUP * bq

    @pl.when(kj == 0)
    def _():
        q = q_ref[0]
        lane = lax.broadcasted_iota(jnp.int32, (bq, LANES), 1)
        for g in range(DA_GROUP):
            qg = q[:, g * LANES:(g + 1) * LANES] * (DA_DH ** -0.5)
            q_sc[0, g * bq:(g + 1) * bq, :] = jnp.where(lane < DA_DH, qg, 0).astype(BF16)
            q_sc[1, g * bq:(g + 1) * bq, :] = jnp.where(lane >= DA_DH, qg, 0).astype(BF16)
        m_sc[...] = jnp.full(m_sc.shape, NEG_BIG, F32)
        l_sc[...] = jnp.zeros_like(l_sc)
        acc_sc[...] = jnp.zeros_like(acc_sc)

    @pl.when(kj * bk <= qi * bq + bq - 1)
    def _():
        k = k_ref[0]
        v = v_ref[0]
        r = lax.broadcasted_iota(jnp.int32, (rows2, bk), 0)
        cidx = lax.broadcasted_iota(jnp.int32, (rows2, bk), 1)
        qpos = qi * bq + jnp.where(r >= bq, r - bq, r)
        mask = qpos >= kj * bk + cidx
        for i in range(2):
            s = jnp.where(mask, _dot_nt(q_sc[i], k), NEG_BIG)
            m_prev = m_sc[i]
            m_new = jnp.maximum(m_prev, jnp.max(s, axis=-1, keepdims=True))
            alpha = jnp.exp(m_prev - m_new)
            p = jnp.exp(s - jnp.concatenate([m_new] * (bk // LANES), axis=1))
            l_sc[i] = alpha * l_sc[i] + jnp.sum(p, axis=-1, keepdims=True)
            acc_sc[i] = alpha * acc_sc[i] + _dot(p.astype(BF16), v)
            m_sc[i] = m_new

    @pl.when(kj == pl.num_programs(3) - 1)
    def _():
        lam = lam_ref[0]
        post = lam_ref[1]
        for g in range(DA_GROUP):
            rs = slice(g * bq, (g + 1) * bq)
            o = acc_sc[0, rs, :] / l_sc[0, rs, :] - lam * (acc_sc[1, rs, :] / l_sc[1, rs, :])
            ms = jnp.mean(o * o, axis=-1, keepdims=True)
            o = o * lax.rsqrt(ms + RMS_EPS) * sw_ref[...] * post
            o_ref[0, :, g * LANES:(g + 1) * LANES] = o.astype(o_ref.dtype)


def _flash(lam2, q, k, v, sw, bq, bk):
    b, s, _ = q.shape
    bq, bk = min(bq, s), min(bk, s)
    nq, nk = s // bq, s // bk
    last = lambda qi: (qi * bq + bq - 1) // bk
    qw = DA_GROUP * DA_DV
    return pl.pallas_call(
        functools.partial(_flash_kernel, bq=bq, bk=bk),
        grid=(b, DA_KV_HEADS, nq, nk),
        in_specs=[pl.BlockSpec(memory_space=pltpu.SMEM),
                  pl.BlockSpec((1, bq, qw), lambda bi, h, qi, kj: (bi, qi, h)),
                  pl.BlockSpec((1, bk, DA_DV), lambda bi, h, qi, kj: (bi, jnp.minimum(kj, last(qi)), h)),
                  pl.BlockSpec((1, bk, DA_DV), lambda bi, h, qi, kj: (bi, jnp.minimum(kj, last(qi)), h)),
                  pl.BlockSpec((1, DA_DV), lambda bi, h, qi, kj: (0, 0))],
        out_specs=pl.BlockSpec((1, bq, qw), lambda bi, h, qi, kj: (bi, qi, h)),
        out_shape=jax.ShapeDtypeStruct((b, s, DA_HEADS * DA_DV), BF16),
        scratch_shapes=[pltpu.VMEM((2, DA_GROUP * bq, LANES), BF16),
                        pltpu.VMEM((2, DA_GROUP * bq, LANES), F32),
                        pltpu.VMEM((2, DA_GROUP * bq, LANES), F32),
                        pltpu.VMEM((2, DA_GROUP * bq, LANES), F32)],
        compiler_params=_params(("parallel", "parallel", "parallel", "arbitrary")),
    )(lam2, q, k, v, sw)


def _paged_kernel(pt_ref, lay_ref, lam_ref, q0_ref, q1_ref, *rest, n_pages):
    kp = rest[:PAGES_PER_STEP]
    vp = rest[PAGES_PER_STEP:2 * PAGES_PER_STEP]
    knew_ref, vnew_ref, nmask_ref, bmask_ref, sw_ref, o_ref, m_sc, l_sc, acc_sc = rest[2 * PAGES_PER_STEP:]
    step = pl.program_id(1)
    n_steps = n_pages // PAGES_PER_STEP

    @pl.when(step == 0)
    def _():
        m_sc[...] = jnp.full(m_sc.shape, NEG_BIG, F32)
        l_sc[...] = jnp.zeros_like(l_sc)
        acc_sc[...] = jnp.zeros_like(acc_sc)

    def update(k, v, add_mask):
        for i, q_ref in enumerate((q0_ref, q1_ref)):
            s = _dot(k, q_ref[0])
            if add_mask is not None:
                s = s + add_mask
            m_prev = m_sc[i]
            m_new = jnp.maximum(m_prev, jnp.max(s, axis=0, keepdims=True))
            alpha = jnp.exp(m_prev - m_new)
            p = jnp.exp(s - m_new)
            l_sc[i] = alpha * l_sc[i] + jnp.sum(p, axis=0, keepdims=True)
            acc_sc[i] = alpha * acc_sc[i] + _dot_tn(v, p.astype(BF16))
            m_sc[i] = m_new

    @pl.when(step < n_steps)
    def _():
        k = jnp.concatenate([r[0, 0] for r in kp], axis=0).astype(BF16)
        v = jnp.concatenate([r[0, 0] for r in vp], axis=0).astype(BF16)
        update(k, v, None)

    @pl.when(step == n_steps)
    def _():
        update(knew_ref[0].astype(BF16), vnew_ref[0].astype(BF16), nmask_ref[...])
        lam = lam_ref[0]
        post = lam_ref[1]
        o = acc_sc[0] / l_sc[0] - lam * (acc_sc[1] / l_sc[1])
        o = o * bmask_ref[...]
        ms = jnp.sum(o * o, axis=0, keepdims=True) * (1.0 / DA_DV)
        o_ref[0] = o * lax.rsqrt(ms + RMS_EPS) * sw_ref[...] * post


def _paged(page_table, layer, lam2, q0, q1, cache_k, cache_v, knew, vnew, nmask, bmask, sw_x):
    nb, n_pages = page_table.shape
    n_steps = n_pages // PAGES_PER_STEP
    kvw = cache_k.shape[-1]
    ncol = q0.shape[-1]

    def page_spec(o):
        def imap(b, s, pt, lay):
            return (lay[0], pt[b, jnp.minimum(s, n_steps - 1) * PAGES_PER_STEP + o], 0, 0)
        return pl.BlockSpec((1, 1, PAGE_SIZE, kvw), imap)

    per_b = lambda a: pl.BlockSpec((1,) + a.shape[1:], lambda b, s, pt, lay: (b, 0, 0))
    full = lambda a: pl.BlockSpec(a.shape, lambda b, s, pt, lay: (0, 0))
    in_specs = ([pl.BlockSpec(memory_space=pltpu.SMEM), per_b(q0), per_b(q1)]
                + [page_spec(o) for o in range(PAGES_PER_STEP)]
                + [page_spec(o) for o in range(PAGES_PER_STEP)]
                + [per_b(knew), per_b(vnew), full(nmask), full(bmask), full(sw_x)])
    grid_spec = pltpu.PrefetchScalarGridSpec(
        num_scalar_prefetch=2, grid=(nb, n_steps + 1), in_specs=in_specs,
        out_specs=pl.BlockSpec((1, kvw, ncol), lambda b, s, pt, lay: (b, 0, 0)),
        scratch_shapes=[pltpu.VMEM((2, 1, ncol), F32), pltpu.VMEM((2, 1, ncol), F32),
                        pltpu.VMEM((2, kvw, ncol), F32)])
    return pl.pallas_call(
        functools.partial(_paged_kernel, n_pages=n_pages),
        grid_spec=grid_spec,
        out_shape=jax.ShapeDtypeStruct((nb, kvw, ncol), F32),
        compiler_params=_params(("parallel", "arbitrary")),
    )(page_table, layer, lam2, q0, q1, *([cache_k] * PAGES_PER_STEP), *([cache_v] * PAGES_PER_STEP),
      knew, vnew, nmask, bmask, sw_x)


def _rope_tables(pos):
    half = ROPE_DIM // 2
    inv_freq = ROPE_THETA ** (-jnp.arange(half, dtype=F32) / half)
    ang = pos.astype(F32)[:, None] * inv_freq[None, :]
    cos, sin = jnp.cos(ang), jnp.sin(ang)
    t = pos.shape[0]
    pad = jnp.zeros((t, DA_DH - ROPE_DIM), F32)
    c = jnp.concatenate([cos, cos, jnp.ones_like(pad)], axis=1)
    s1 = jnp.concatenate([-sin, jnp.zeros_like(sin), pad], axis=1)
    s2 = jnp.concatenate([jnp.zeros_like(sin), sin, pad], axis=1)
    rep = LANES // DA_DH
    return tuple(jnp.tile(a, (1, rep)) for a in (c, s1, s2))


def _ssd_consts():
    q = SSD_CHUNK
    tri = jnp.asarray(np.tril(np.ones((q, q), np.float32)), BF16)
    eye = jnp.asarray(np.eye(LANES, dtype=np.float32), BF16)
    expand = np.zeros((LANES, SSD_INNER), np.float32)
    for j in range(SSD_HEADS):
        expand[j, j * SSD_HEADDIM:(j + 1) * SSD_HEADDIM] = 1.0
    return tri, eye, jnp.asarray(expand, BF16)


def _pad_cols(a, width):
    return jnp.pad(a, [(0, 0)] * (a.ndim - 1) + [(0, width - a.shape[-1])])


def _mixer_inputs(x2, wl, rope, bm):
    z = _mm(x2, wl['w_z'], BF16, bm, 1024)
    xbc = _mm(x2, wl['w_xbc'], BF16, bm, 1024)
    dtr = _mm(x2, wl['w_dt'], F32, bm, LANES)
    q = _mm(x2, wl['w_q'], BF16, bm, 1024, rope)
    k = _mm(x2, wl['w_k'], F32, bm, 512, rope)
    v = _mm(x2, wl['w_v'], F32, bm, 512)
    g = _mm(x2, wl['w_g'], BF16, bm, 1024)
    return z, xbc, dtr, q, k, v, g


def _post_mixer(x2, y, o, g, mk, mv, wl, bm, tq):
    nb = mk.shape[0]
    x1 = _merge(y, o, g, x2, wl['w_ssd_br'], wl['w_att_br'], wl['w_o'], wl['ln1_g'], wl['ln1_b'], bm)
    cq = _mm(x1, wl['w_cq'], BF16, bm, 1024)
    t = cq.shape[0] // nb
    if t < 16:
        cq3 = jnp.pad(cq.reshape(nb, t, D_MODEL), ((0, 0), (0, 16 - t), (0, 0)))
        co = _cross(cq3, mk, mv, 16)[:, :t].reshape(nb * t, D_MODEL)
    else:
        co = _cross(cq.reshape(nb, t, D_MODEL), mk, mv, tq).reshape(nb * t, D_MODEL)
    x2n = _mm_res_ln(co, wl['w_co'], x1, wl['ln2_g'], wl['ln2_b'], bm)
    return _mlp(x2n, wl['w_up'], wl['w_down'], wl['ln3_g'], wl['ln3_b'], bm, 1024)


def kernel(x_prompt, x_sample, mem_prompt, cache_k, cache_v, state_ssm, state_conv, cache_mem_k, cache_mem_v, page_table, w_in, conv_w, conv_b, dt_bias, a_log, d_skip, ssd_norm_w, w_ssd_br, lambda_q, lambda_k, subln_w, w_att_br, w_o, ln1_g, ln1_b, w_cq, w_ck, w_cv, w_co, ln2_g, ln2_b, w_up, w_down, ln3_g, ln3_b):
    nbp, seq, _ = x_prompt.shape
    nbs, tdec, _ = x_sample.shape
    n_pool = cache_k.shape[1]
    kvw = DA_KV_HEADS * DA_DV

    sp = tuple(int(v) for v in np.cumsum(IN_SPLITS))
    w_in_b = w_in.astype(BF16)
    rope_p = _rope_tables(jnp.arange(seq, dtype=jnp.int32))
    rope_s = _rope_tables(jnp.tile(PAST_LEN + jnp.arange(tdec, dtype=jnp.int32), nbs))
    consts = _ssd_consts()

    cache_k4 = cache_k.reshape(DEPTH, n_pool, PAGE_SIZE, kvw)
    cache_v4 = cache_v.reshape(DEPTH, n_pool, PAGE_SIZE, kvw)

    ncol = DA_KV_HEADS * tdec * DA_GROUP
    col_t = (np.arange(ncol) % (tdec * DA_GROUP)) // DA_GROUP
    col_h = np.arange(ncol) // (tdec * DA_GROUP)
    nmask = jnp.asarray(np.where(np.arange(NEW_ROWS)[:, None] <= col_t[None, :], 0.0, NEG_BIG), F32)
    bmask = jnp.asarray((np.arange(kvw)[:, None] // DA_DV == col_h[None, :]).astype(np.float32))

    xp = x_prompt.reshape(nbp * seq, D_MODEL)
    xs = x_sample.reshape(nbs * tdec, D_MODEL)
    mem2 = mem_prompt.reshape(nbp * N_MEM, D_MODEL)
    zeros_h = jnp.zeros((nbp, SSD_INNER, D_STATE), F32)
    zeros_c = jnp.zeros((nbp, 8, CONV_DIM), F32)

    outs = [[] for _ in range(10)]
    for l in range(DEPTH):
        wl = {
            'w_z': w_in_b[l, :, :sp[0]], 'w_xbc': w_in_b[l, :, sp[0]:sp[1]],
            'w_dt': _pad_cols(w_in_b[l, :, sp[1]:sp[2]], LANES),
            'w_q': w_in_b[l, :, sp[2]:sp[3]], 'w_k': w_in_b[l, :, sp[3]:sp[4]],
            'w_v': w_in_b[l, :, sp[4]:sp[5]], 'w_g': w_in_b[l, :, sp[5]:sp[6]],
            'w_ssd_br': w_ssd_br[l].astype(BF16), 'w_att_br': w_att_br[l].astype(BF16),
            'w_o': w_o[l].astype(BF16), 'w_cq': w_cq[l].astype(BF16), 'w_co': w_co[l].astype(BF16),
            'w_up': w_up[l].astype(BF16), 'w_down': w_down[l].astype(BF16),
            'ln1_g': ln1_g[l][None], 'ln1_b': ln1_b[l][None], 'ln2_g': ln2_g[l][None],
            'ln2_b': ln2_b[l][None], 'ln3_g': ln3_g[l][None], 'ln3_b': ln3_b[l][None],
        }
        ssd_p = (conv_w[l], conv_b[l][None], _pad_cols(dt_bias[l][None], LANES),
                 _pad_cols(a_log[l][None], LANES), jnp.repeat(d_skip[l], SSD_HEADDIM)[None],
                 ssd_norm_w[l][None], consts)
        lam_init = 0.8 - 0.6 * math.exp(-0.3 * l)
        lq = lambda_q[l].astype(F32)
        lk = lambda_k[l].astype(F32)
        lam = jnp.exp(jnp.sum(lq[0] * lk[0])) - jnp.exp(jnp.sum(lq[1] * lk[1])) + lam_init
        lam2 = jnp.stack([lam, jnp.asarray(1.0 - lam_init, F32)]).astype(F32)
        sw = subln_w[l][None]

        mk_p = _mm(mem2, w_ck[l].astype(BF16), F32, 512, 1024)
        mv_p = _mm(mem2, w_cv[l].astype(BF16), F32, 512, 1024)
        z, xbc, dtr, q, k, v, g = _mixer_inputs(xp, wl, rope_p, 1024)
        y, h_p = _ssd(xbc.reshape(nbp, seq, CONV_DIM), dtr.reshape(nbp, seq, LANES),
                      z.reshape(nbp, seq, SSD_INNER), zeros_h, zeros_c, *ssd_p, valid=SSD_CHUNK)
        o = _flash(lam2, q.reshape(nbp, seq, -1), k.astype(BF16).reshape(nbp, seq, kvw),
                   v.astype(BF16).reshape(nbp, seq, kvw), sw, 512, 512)
        xp = _post_mixer(xp, y.reshape(nbp * seq, SSD_INNER), o.reshape(nbp * seq, D_MODEL), g,
                         mk_p.astype(BF16).reshape(nbp, N_MEM, D_MODEL),
                         mv_p.astype(BF16).reshape(nbp, N_MEM, D_MODEL), wl, 512, 512)
        c_p = xbc.reshape(nbp, seq, CONV_DIM)[:, seq - (CONV_W - 1):].astype(F32)

        zs, xbcs, dtrs, qs, ks, vs, gs = _mixer_inputs(xs, wl, rope_s, LANES)
        pad_t = lambda a: jnp.pad(a.reshape(nbs, tdec, -1), ((0, 0), (0, SSD_CHUNK - tdec), (0, 0)))
        cprev = jnp.pad(state_conv[l], ((0, 0), (8 - (CONV_W - 1), 0), (0, 0)))
        ys, h_s = _ssd(pad_t(xbcs), pad_t(dtrs), pad_t(zs),
                       state_ssm[l].reshape(nbs, SSD_INNER, D_STATE), cprev, *ssd_p, valid=tdec)
        ys = ys[:, :tdec].reshape(nbs * tdec, SSD_INNER)
        c_s = jnp.concatenate([state_conv[l], xbcs.reshape(nbs, tdec, CONV_DIM).astype(F32)],
                              axis=1)[:, -(CONV_W - 1):]

        q6 = qs.astype(F32).reshape(nbs, tdec, DA_KV_HEADS, DA_GROUP, 2, DA_DH) * (DA_DH ** -0.5)
        q6 = jnp.transpose(q6, (0, 4, 2, 5, 1, 3))
        eye_h = jnp.eye(DA_KV_HEADS, dtype=F32)
        qsel = []
        for i in range(2):
            blk = jnp.einsum('bhdtg,hk->bhdktg', q6[:, i], eye_h)
            blk = jnp.pad(blk, ((0, 0), (0, 0), (i * DA_DH, (1 - i) * DA_DH), (0, 0), (0, 0), (0, 0)))
            qsel.append(blk.reshape(nbs, kvw, ncol).astype(BF16))
        knew = jnp.pad(ks.reshape(nbs, tdec, kvw), ((0, 0), (0, NEW_ROWS - tdec), (0, 0)))
        vnew = jnp.pad(vs.reshape(nbs, tdec, kvw), ((0, 0), (0, NEW_ROWS - tdec), (0, 0)))
        sw_x = jnp.broadcast_to(jnp.tile(subln_w[l], DA_KV_HEADS)[:, None], (kvw, ncol))
        o_t = _paged(page_table, jnp.full((1,), l, jnp.int32), lam2, qsel[0], qsel[1],
                     cache_k4, cache_v4, knew, vnew, nmask, bmask, sw_x)
        o_s = o_t.reshape(nbs, DA_KV_HEADS, DA_DV, DA_KV_HEADS, tdec, DA_GROUP).sum(axis=1)
        o_s = jnp.transpose(o_s, (0, 3, 2, 4, 1)).reshape(nbs * tdec, D_MODEL).astype(BF16)
        xs = _post_mixer(xs, ys, o_s, gs, cache_mem_k[l].astype(BF16).reshape(nbs, N_MEM, D_MODEL),
                         cache_mem_v[l].astype(BF16).reshape(nbs, N_MEM, D_MODEL), wl, LANES, 16)

        vals = (k.reshape(nbp, seq, DA_KV_HEADS, 2, DA_DH), v.reshape(nbp, seq, DA_KV_HEADS, DA_DV),
                h_p.reshape(nbp, SSD_HEADS, SSD_HEADDIM, D_STATE), c_p,
                mk_p.reshape(nbp, N_MEM, CA_HEADS, CA_DH), mv_p.reshape(nbp, N_MEM, CA_HEADS, CA_DH),
                ks.reshape(nbs, tdec, DA_KV_HEADS, 2, DA_DH), vs.reshape(nbs, tdec, DA_KV_HEADS, DA_DV),
                h_s.reshape(nbs, SSD_HEADS, SSD_HEADDIM, D_STATE), c_s)
        for lst, val in zip(outs, vals):
            lst.append(val)

    return (xp.reshape(nbp, seq, D_MODEL), xs.reshape(nbs, tdec, D_MODEL)) + tuple(jnp.stack(o) for o in outs)
```

```python
import functools
import math

import jax
import jax.numpy as jnp
import numpy as np
from jax import lax
from jax.experimental import pallas as pl
from jax.experimental.pallas import tpu as pltpu

F32 = jnp.float32
BF16 = jnp.bfloat16

D_MODEL = 1024
DEPTH = 4
PAST_LEN = 8192
PAGE_SIZE = 128
SSD_INNER = 2 * D_MODEL
SSD_HEADDIM = 64
SSD_HEADS = SSD_INNER // SSD_HEADDIM
SSD_GROUPS = 4
SSD_HPG = SSD_HEADS // SSD_GROUPS
D_STATE = 128
CONV_W = 4
CONV_DIM = SSD_INNER + 2 * SSD_GROUPS * D_STATE
SSD_CHUNK = 128
DA_HEADS = 8
DA_KV_HEADS = 4
DA_GROUP = DA_HEADS // DA_KV_HEADS
DA_DH = 64
DA_DV = 2 * DA_DH
ROPE_DIM = DA_DH // 4
ROPE_THETA = 500000.0
N_MEM = 256
CA_HEADS = 4
CA_DH = D_MODEL // CA_HEADS
D_FF = 4 * D_MODEL
IN_SPLITS = (SSD_INNER, CONV_DIM, SSD_HEADS, DA_HEADS * 2 * DA_DH,
             DA_KV_HEADS * 2 * DA_DH, DA_KV_HEADS * DA_DV, 2 * D_MODEL)
DN_ALPHA = (2 * DEPTH) ** 0.25
LN_EPS = 1e-5
RMS_EPS = 1e-6

LANES = 128
NEG_BIG = -1e30
VMEM_LIMIT = 56 * 1024 * 1024
PAGES_PER_STEP = 8
FLASH_BQ = 1024
FLASH_BK = 512

NT_DIMS = (((1,), (1,)), ((), ()))
TN_DIMS = (((0,), (0,)), ((), ()))


def _params(sem):
    return pltpu.CompilerParams(dimension_semantics=sem, vmem_limit_bytes=VMEM_LIMIT)


def _dot(a, b):
    return jnp.dot(a, b, preferred_element_type=F32)


def _dot_nt(a, b):
    return lax.dot_general(a, b, NT_DIMS, preferred_element_type=F32)


def _dot_tn(a, b):
    return lax.dot_general(a, b, TN_DIMS, preferred_element_type=F32)


def _split3(a):
    hi = a.astype(BF16)
    r = a - hi.astype(F32)
    mid = r.astype(BF16)
    lo = (r - mid.astype(F32)).astype(BF16)
    return hi, mid, lo


def _sigmoid(x):
    return 1.0 / (1.0 + jnp.exp(-x))


def _layer_norm(v, g, b):
    mu = jnp.mean(v, axis=-1, keepdims=True)
    d = v - mu
    var = jnp.mean(d * d, axis=-1, keepdims=True)
    return d * lax.rsqrt(var + LN_EPS) * g + b


def _mm_kernel(x_ref, w_ref, o_ref):
    o_ref[...] = _dot(x_ref[...].astype(BF16), w_ref[...]).astype(o_ref.dtype)


def _mm_rope_kernel(x_ref, w_ref, c_ref, s1_ref, s2_ref, o_ref):
    acc = _dot(x_ref[...].astype(BF16), w_ref[...])
    c, s1, s2 = c_ref[...], s1_ref[...], s2_ref[...]
    for cb in range(acc.shape[1] // LANES):
        blk = acc[:, cb * LANES:(cb + 1) * LANES]
        up = pltpu.roll(blk, LANES - ROPE_DIM // 2, 1)
        dn = pltpu.roll(blk, ROPE_DIM // 2, 1)
        o_ref[:, cb * LANES:(cb + 1) * LANES] = (blk * c + up * s1 + dn * s2).astype(o_ref.dtype)


def _mm(x, w, out_dtype, bm, bn, rope=None):
    m, k = x.shape
    n = w.shape[1]
    bm, bn = min(bm, m), min(bn, n)
    if rope is not None:
        bm = min(bm, rope[0].shape[0])
    in_specs = [pl.BlockSpec((bm, k), lambda i, j: (i, 0)),
                pl.BlockSpec((k, bn), lambda i, j: (0, j))]
    args = [x, w]
    body = _mm_kernel
    if rope is not None:
        nrep = rope[0].shape[0] // bm
        for t in rope:
            in_specs.append(pl.BlockSpec((bm, LANES), lambda i, j: (i % nrep, 0)))
            args.append(t)
        body = _mm_rope_kernel
    return pl.pallas_call(
        body, grid=(m // bm, n // bn), in_specs=in_specs,
        out_specs=pl.BlockSpec((bm, bn), lambda i, j: (i, j)),
        out_shape=jax.ShapeDtypeStruct((m, n), out_dtype),
        compiler_params=_params(("parallel", "parallel")), name="proj",
    )(*args)


def _mm_res_ln_kernel(a_ref, w_ref, x_ref, g_ref, b_ref, o_ref):
    y = _dot(a_ref[...], w_ref[...])
    o_ref[...] = _layer_norm(DN_ALPHA * x_ref[...] + y, g_ref[...], b_ref[...])


def _mm_res_ln(a, w, x, g, b, bm):
    m, k = a.shape
    n = w.shape[1]
    bm = min(bm, m)
    return pl.pallas_call(
        _mm_res_ln_kernel, grid=(m // bm,),
        in_specs=[pl.BlockSpec((bm, k), lambda i: (i, 0)),
                  pl.BlockSpec((k, n), lambda i: (0, 0)),
                  pl.BlockSpec((bm, n), lambda i: (i, 0)),
                  pl.BlockSpec((1, n), lambda i: (0, 0)),
                  pl.BlockSpec((1, n), lambda i: (0, 0))],
        out_specs=pl.BlockSpec((bm, n), lambda i: (i, 0)),
        out_shape=jax.ShapeDtypeStruct((m, n), F32),
        compiler_params=_params(("parallel",)), name="proj_res_ln",
    )(a, w, x, g, b)


def _merge_kernel(y_ref, o_ref, g_ref, x_ref, wbr_ref, watt_ref, wo_ref, lg_ref, lb_ref, out_ref):
    ssd = _dot(y_ref[...], wbr_ref[...])
    att = _dot(o_ref[...], watt_ref[...])
    g = g_ref[...].astype(F32)
    merged = _sigmoid(g[:, :D_MODEL]) * ssd + _sigmoid(g[:, D_MODEL:]) * att
    mix = _dot(merged.astype(BF16), wo_ref[...])
    out_ref[...] = _layer_norm(DN_ALPHA * x_ref[...] + mix, lg_ref[...], lb_ref[...])


def _merge(y, o, g, x, wbr, watt, wo, lg, lb, bm):
    m = x.shape[0]
    bm = min(bm, m)
    row = lambda w: pl.BlockSpec((bm, w), lambda i: (i, 0))
    full = lambda a: pl.BlockSpec(a.shape, lambda i: (0, 0))
    return pl.pallas_call(
        _merge_kernel, grid=(m // bm,),
        in_specs=[row(SSD_INNER), row(D_MODEL), row(2 * D_MODEL), row(D_MODEL),
                  full(wbr), full(watt), full(wo), full(lg), full(lb)],
        out_specs=row(D_MODEL),
        out_shape=jax.ShapeDtypeStruct((m, D_MODEL), F32),
        compiler_params=_params(("parallel",)), name="merge",
    )(y, o, g, x, wbr, watt, wo, lg, lb)


def _mlp_kernel(x_ref, wu_ref, wd_ref, lg_ref, lb_ref, o_ref, acc_sc, xb_sc):
    j = pl.program_id(1)

    @pl.when(j == 0)
    def _():
        acc_sc[...] = jnp.zeros_like(acc_sc)
        xb_sc[...] = x_ref[...].astype(BF16)

    h = jnp.maximum(_dot(xb_sc[...], wu_ref[...]), 0.0)
    acc_sc[...] += _dot((h * h).astype(BF16), wd_ref[...])

    @pl.when(j == pl.num_programs(1) - 1)
    def _():
        o_ref[...] = _layer_norm(DN_ALPHA * x_ref[...] + acc_sc[...], lg_ref[...], lb_ref[...])


def _mlp(x, wu, wd, lg, lb, bm, bf):
    m = x.shape[0]
    bm = min(bm, m)
    return pl.pallas_call(
        _mlp_kernel, grid=(m // bm, D_FF // bf),
        in_specs=[pl.BlockSpec((bm, D_MODEL), lambda i, j: (i, 0)),
                  pl.BlockSpec((D_MODEL, bf), lambda i, j: (0, j)),
                  pl.BlockSpec((bf, D_MODEL), lambda i, j: (j, 0)),
                  pl.BlockSpec((1, D_MODEL), lambda i, j: (0, 0)),
                  pl.BlockSpec((1, D_MODEL), lambda i, j: (0, 0))],
        out_specs=pl.BlockSpec((bm, D_MODEL), lambda i, j: (i, 0)),
        out_shape=jax.ShapeDtypeStruct((m, D_MODEL), F32),
        scratch_shapes=[pltpu.VMEM((bm, D_MODEL), F32), pltpu.VMEM((bm, D_MODEL), BF16)],
        compiler_params=_params(("parallel", "arbitrary")), name="mlp",
    )(x, wu, wd, lg, lb)


def _cross_kernel(q_ref, k_ref, v_ref, o_ref):
    q = q_ref[0]
    k = k_ref[0]
    v = v_ref[0]
    for h in range(CA_HEADS):
        sl = slice(h * CA_DH, (h + 1) * CA_DH)
        s = _dot_nt(q[:, sl], k[:, sl]) * (CA_DH ** -0.5)
        s = s - jnp.max(s, axis=-1, keepdims=True)
        p = jnp.exp(s)
        p = p / jnp.sum(p, axis=-1, keepdims=True)
        o_ref[0, :, sl] = _dot(p.astype(BF16), v[:, sl]).astype(o_ref.dtype)


def _cross(q, mk, mv, tq):
    b, t, d = q.shape
    tq = min(tq, t)
    return pl.pallas_call(
        _cross_kernel, grid=(b, t // tq),
        in_specs=[pl.BlockSpec((1, tq, d), lambda i, j: (i, j, 0)),
                  pl.BlockSpec((1, N_MEM, d), lambda i, j: (i, 0, 0)),
                  pl.BlockSpec((1, N_MEM, d), lambda i, j: (i, 0, 0))],
        out_specs=pl.BlockSpec((1, tq, d), lambda i, j: (i, j, 0)),
        out_shape=jax.ShapeDtypeStruct((b, t, d), BF16),
        compiler_params=_params(("parallel", "parallel")), name="cross_attn",
    )(q, mk, mv)


def _ssd_kernel(xbc_ref, dt_ref, z_ref, h0_ref, cprev_ref, cw_ref, cb_ref, dtb_ref, alog_ref,
                dskip_ref, nw_ref, tri_ref, eye_ref, exp_ref, y_ref, hout_ref,
                h_sc, xpad_sc, y_sc, *, q_len, valid):
    c = pl.program_id(1)

    @pl.when(c == 0)
    def _():
        h_sc[...] = h0_ref[0]
        xpad_sc[0:8, :] = cprev_ref[0]

    xpad_sc[8:8 + q_len, :] = xbc_ref[0].astype(F32)
    cw = cw_ref[...]
    conv = cb_ref[...] + cw[3:4, :] * xpad_sc[8:8 + q_len, :]
    for kk in range(1, CONV_W):
        conv = conv + cw[3 - kk:4 - kk, :] * xpad_sc[8 - kk:8 - kk + q_len, :]
    xpad_sc[0:8, :] = xpad_sc[q_len:q_len + 8, :]
    xbc = conv * _sigmoid(conv)
    xs = xbc[:, :SSD_INNER]

    dtr = dt_ref[0] + dtb_ref[...]
    dtv = jnp.maximum(dtr, 0.0) + jnp.log(1.0 + jnp.exp(-jnp.abs(dtr)))
    if valid < q_len:
        rows = lax.broadcasted_iota(jnp.int32, dtv.shape, 0)
        dtv = jnp.where(rows < valid, dtv, 0.0)
    a = dtv * (-jnp.exp(alog_ref[...]))

    tri = tri_ref[...]
    eye = eye_ref[...]
    expand = exp_ref[...]
    a_parts = _split3(a)
    acum = sum(_dot(tri, p) for p in a_parts)
    a_t = sum(_dot_nt(eye, p) for p in a_parts)
    acum_t = sum(_dot_nt(p, tri) for p in _split3(a_t))
    acum_x = sum(_dot(p, expand) for p in _split3(acum))
    dt_x = sum(_dot(p, expand) for p in _split3(dtv))

    xdt = xs * dt_x
    xdt_b = xdt.astype(BF16)
    xdec = (xdt * jnp.exp(acum_x[q_len - 1:q_len, :] - acum_x)).astype(BF16)
    eacum_x = jnp.exp(acum_x)
    chunk_decay = jnp.exp(jnp.broadcast_to(acum_t[:, q_len - 1:q_len], (LANES, LANES)))

    li = lax.broadcasted_iota(jnp.int32, (q_len, q_len), 0)
    si = lax.broadcasted_iota(jnp.int32, (q_len, q_len), 1)
    causal = li >= si
    left = lax.broadcasted_iota(jnp.int32, (q_len, LANES), 1) < SSD_HEADDIM
    gw = SSD_HPG * SSD_HEADDIM
    for g in range(SSD_GROUPS):
        b_g = xbc[:, SSD_INNER + g * D_STATE:SSD_INNER + (g + 1) * D_STATE].astype(BF16)
        c_off = SSD_INNER + SSD_GROUPS * D_STATE
        c_g = xbc[:, c_off + g * D_STATE:c_off + (g + 1) * D_STATE].astype(BF16)
        cbm = _dot_nt(c_g, b_g)
        h_g = h_sc[g * gw:(g + 1) * gw, :]
        y_off = _dot_nt(c_g, h_g.astype(BF16))
        for pp in range(SSD_HPG // 2):
            j0 = g * SSD_HPG + 2 * pp
            cols = slice(j0 * SSD_HEADDIM, (j0 + 2) * SSD_HEADDIM)
            x_pair = xdt_b[:, cols]
            halves = []
            for j in (j0, j0 + 1):
                seg = acum[:, j:j + 1] - acum_t[j:j + 1, :]
                m_j = (cbm * jnp.where(causal, jnp.exp(seg), 0.0)).astype(BF16)
                halves.append(_dot(m_j, x_pair))
            y_diag = jnp.where(left, halves[0], halves[1])
            y_sc[:, cols] = (y_diag + y_off[:, pp * LANES:(pp + 1) * LANES] * eacum_x[:, cols]
                             + xs[:, cols] * dskip_ref[:, cols])
        new_states = _dot_tn(xdec[:, g * gw:(g + 1) * gw], b_g)
        for jj in range(SSD_HPG):
            j = g * SSD_HPG + jj
            r = slice(j * SSD_HEADDIM, (j + 1) * SSD_HEADDIM)
            rl = slice(jj * SSD_HEADDIM, (jj + 1) * SSD_HEADDIM)
            h_sc[r, :] = h_g[rl, :] * chunk_decay[j:j + 1, :] + new_states[rl, :]

    z = z_ref[0].astype(F32)
    yz = y_sc[...] * (z * _sigmoid(z))
    ms = jnp.mean(yz * yz, axis=-1, keepdims=True)
    y_ref[0] = (yz * lax.rsqrt(ms + RMS_EPS) * nw_ref[...]).astype(y_ref.dtype)

    @pl.when(c == pl.num_programs(1) - 1)
    def _():
        hout_ref[0] = h_sc[...]


def _ssd(xbc, dtr, z, h0, cprev, cw, cb, dtb, alog, dskip_x, nw, consts, valid):
    b, l, _ = xbc.shape
    q_len = SSD_CHUNK
    nc = l // q_len
    tri, eye, expand = consts
    tok = lambda w: pl.BlockSpec((1, q_len, w), lambda i, c: (i, c, 0))
    per_b = lambda r, w: pl.BlockSpec((1, r, w), lambda i, c: (i, 0, 0))
    full = lambda a: pl.BlockSpec(a.shape, lambda i, c: (0, 0))
    return pl.pallas_call(
        functools.partial(_ssd_kernel, q_len=q_len, valid=valid),
        grid=(b, nc),
        in_specs=[tok(CONV_DIM), tok(LANES), tok(SSD_INNER),
                  per_b(SSD_INNER, D_STATE), per_b(8, CONV_DIM),
                  full(cw), full(cb), full(dtb), full(alog), full(dskip_x), full(nw),
                  full(tri), full(eye), full(expand)],
        out_specs=[tok(SSD_INNER), per_b(SSD_INNER, D_STATE)],
        out_shape=[jax.ShapeDtypeStruct((b, l, SSD_INNER), BF16),
                   jax.ShapeDtypeStruct((b, SSD_INNER, D_STATE), F32)],
        scratch_shapes=[pltpu.VMEM((SSD_INNER, D_STATE), F32),
                        pltpu.VMEM((q_len + 8, CONV_DIM), F32),
                        pltpu.VMEM((q_len, SSD_INNER), F32)],
        compiler_params=_params(("parallel", "arbitrary")), name="ssd",
    )(xbc, dtr, z, h0, cprev, cw, cb, dtb, alog, dskip_x, nw, tri, eye, expand)


def _flash_kernel(qm_ref, km_ref, lam_ref, qt_ref, k_ref, vt_ref, sw_ref, o_ref,
                  qw_sc, m_sc, acc_sc, *, bq, bk):
    step = pl.program_id(2)
    qi = qm_ref[step]
    kj = km_ref[step]
    cols = DA_GROUP * bq

    @pl.when(kj == 0)
    def _():
        row = lax.broadcasted_iota(jnp.int32, (LANES, bq), 0)
        for g in range(DA_GROUP):
            qg = qt_ref[0, 0, g].astype(F32)
            qw_sc[0, :, g * bq:(g + 1) * bq] = jnp.where(row < DA_DH, qg, 0.0).astype(BF16)
            qw_sc[1, :, g * bq:(g + 1) * bq] = jnp.where(row >= DA_DH, qg, 0.0).astype(BF16)
        m_sc[...] = jnp.full(m_sc.shape, NEG_BIG, F32)
        acc_sc[...] = jnp.zeros_like(acc_sc)

    def update(masked):
        k = k_ref[0]
        vt = vt_ref[0, 0]
        if masked:
            kpos = kj * bk + lax.broadcasted_iota(jnp.int32, (bk, cols), 0)
            c = lax.broadcasted_iota(jnp.int32, (bk, cols), 1)
            mask = qi * bq + c % bq >= kpos
        for i in range(2):
            st = _dot(k, qw_sc[i])
            if masked:
                st = jnp.where(mask, st, NEG_BIG)
            m_prev = m_sc[i]
            m_new = jnp.maximum(m_prev, jnp.max(st, axis=0, keepdims=True))
            p = jnp.exp2(st - m_new).astype(BF16)
            acc_sc[i] = jnp.exp2(m_prev - m_new) * acc_sc[i] + _dot(vt, p)
            m_sc[i] = m_new

    on_diag = kj * bk + bk - 1 > qi * bq
    pl.when(on_diag)(functools.partial(update, True))
    pl.when(jnp.logical_not(on_diag))(functools.partial(update, False))

    @pl.when(kj == (qi + 1) * (bq // bk) - 1)
    def _():
        lam = lam_ref[0]
        post = lam_ref[1]
        sw = jnp.concatenate([sw_ref[...]] * (bq // LANES), axis=1)
        for g in range(DA_GROUP):
            cs = slice(g * bq, (g + 1) * bq)
            o = (acc_sc[0, :DA_DV, cs] / acc_sc[0, DA_DV:DA_DV + 1, cs]
                 - lam * (acc_sc[1, :DA_DV, cs] / acc_sc[1, DA_DV:DA_DV + 1, cs]))
            ms = jnp.mean(o * o, axis=0, keepdims=True)
            o_ref[0, 0, g] = (o * lax.rsqrt(ms + RMS_EPS) * sw * post).astype(o_ref.dtype)


def _flash(lam2, qt, k, vt, sw_b, bq, bk):
    b, s, _ = k.shape
    dva = vt.shape[2]
    bq, bk = min(bq, s), min(bk, s)
    assert bq % bk == 0 and s % bq == 0 and bq % LANES == 0
    ratio = bq // bk
    pairs = [(qi, kj) for qi in range(s // bq) for kj in range((qi + 1) * ratio)]
    qmap = jnp.asarray([p[0] for p in pairs], jnp.int32)
    kmap = jnp.asarray([p[1] for p in pairs], jnp.int32)
    cols = DA_GROUP * bq
    grid_spec = pltpu.PrefetchScalarGridSpec(
        num_scalar_prefetch=2, grid=(b, DA_KV_HEADS, len(pairs)),
        in_specs=[pl.BlockSpec(memory_space=pltpu.SMEM),
                  pl.BlockSpec((1, 1, DA_GROUP, LANES, bq), lambda bi, h, st, qm, km: (bi, h, 0, 0, qm[st])),
                  pl.BlockSpec((1, bk, LANES), lambda bi, h, st, qm, km: (bi, km[st], h)),
                  pl.BlockSpec((1, 1, dva, bk), lambda bi, h, st, qm, km: (bi, h, 0, km[st])),
                  pl.BlockSpec((DA_DV, LANES), lambda bi, h, st, qm, km: (0, 0))],
        out_specs=pl.BlockSpec((1, 1, DA_GROUP, DA_DV, bq), lambda bi, h, st, qm, km: (bi, h, 0, 0, qm[st])),
        scratch_shapes=[pltpu.VMEM((2, LANES, cols), BF16),
                        pltpu.VMEM((2, 1, cols), F32),
                        pltpu.VMEM((2, dva, cols), F32)])
    return pl.pallas_call(
        functools.partial(_flash_kernel, bq=bq, bk=bk), grid_spec=grid_spec,
        out_shape=jax.ShapeDtypeStruct((b, DA_KV_HEADS, DA_GROUP, DA_DV, s), BF16),
        compiler_params=_params(("parallel", "parallel", "arbitrary")), name="flash_diff_attn",
    )(qmap, kmap, lam2, qt, k, vt, sw_b)


def _paged_kernel(pt_ref, lay_ref, lam_ref, q_ref, *rest, n_pages):
    npg = PAGES_PER_STEP
    kp, vp = rest[:npg], rest[npg:2 * npg]
    knew_ref, vnew_ref, nmask_ref, sw_ref, o_ref, m_sc, l_sc, acc_sc = rest[2 * npg:]
    step = pl.program_id(1)
    n_steps = n_pages // npg
    hrows = q_ref.shape[1] // DA_KV_HEADS

    @pl.when(step == 0)
    def _():
        m_sc[...] = jnp.full(m_sc.shape, NEG_BIG, F32)
        l_sc[...] = jnp.zeros_like(l_sc)
        acc_sc[...] = jnp.zeros_like(acc_sc)

    def update(s, v_of_head):
        m_prev = m_sc[...]
        m_new = jnp.maximum(m_prev, jnp.max(s, axis=-1, keepdims=True))
        alpha = jnp.exp2(m_prev - m_new)
        p = jnp.exp2(s - jnp.concatenate([m_new] * (s.shape[1] // LANES), axis=1))
        l_sc[...] = alpha * l_sc[...] + jnp.sum(p, axis=-1, keepdims=True)
        pb = p.astype(BF16)
        for h in range(DA_KV_HEADS):
            rs = slice(h * hrows, (h + 1) * hrows)
            acc_sc[rs, :] = alpha[rs, :] * acc_sc[rs, :] + _dot(pb[rs, :], v_of_head(h))
        m_sc[...] = m_new

    @pl.when(step < n_steps)
    def _():
        kt = jnp.concatenate([r[0, 0].astype(BF16) for r in kp], axis=1)
        s = _dot(q_ref[0], kt)

        def v_of_head(h):
            return jnp.concatenate(
                [r[0, 0, pl.ds(h, PAGE_SIZE, stride=DA_KV_HEADS), :] for r in vp], axis=0).astype(BF16)

        update(s, v_of_head)

    @pl.when(step == n_steps)
    def _():
        s = _dot(q_ref[0], knew_ref[0].astype(BF16)) + nmask_ref[...]
        update(s, lambda h: vnew_ref[0][:, h * DA_DV:(h + 1) * DA_DV].astype(BF16))
        lam = lam_ref[0]
        post = lam_ref[1]
        half = hrows // 2
        for h in range(DA_KV_HEADS):
            r0 = slice(h * hrows, h * hrows + half)
            r1 = slice(h * hrows + half, (h + 1) * hrows)
            o = acc_sc[r0, :] / l_sc[r0, :] - lam * (acc_sc[r1, :] / l_sc[r1, :])
            ms = jnp.mean(o * o, axis=-1, keepdims=True)
            o_ref[0, h * half:(h + 1) * half, :] = o * lax.rsqrt(ms + RMS_EPS) * sw_ref[...] * post


def _paged(page_table, layer, lam2, qmat, cache_kt, cache_v2, knew_t, vnew, nmask, sw):
    nb, n_pages = page_table.shape
    npg = PAGES_PER_STEP
    assert n_pages % npg == 0
    n_steps = n_pages // npg
    nrows = qmat.shape[1]
    kvw = cache_kt.shape[2]

    def page_spec(o):
        def imap(b, s, pt, lay):
            return (lay[0], pt[b, jnp.minimum(s, n_steps - 1) * npg + o], 0, 0)
        return pl.BlockSpec((1, 1, kvw, LANES), imap)

    per_b = lambda a: pl.BlockSpec((1,) + a.shape[1:], lambda b, s, pt, lay: (b, 0, 0))
    full = lambda a: pl.BlockSpec(a.shape, lambda b, s, pt, lay: (0, 0))
    in_specs = ([pl.BlockSpec(memory_space=pltpu.SMEM), per_b(qmat)]
                + [page_spec(o) for o in range(npg)] + [page_spec(o) for o in range(npg)]
                + [per_b(knew_t), per_b(vnew), full(nmask), full(sw)])
    grid_spec = pltpu.PrefetchScalarGridSpec(
        num_scalar_prefetch=2, grid=(nb, n_steps + 1), in_specs=in_specs,
        out_specs=pl.BlockSpec((1, nrows // 2, DA_DV), lambda b, s, pt, lay: (b, 0, 0)),
        scratch_shapes=[pltpu.VMEM((nrows, LANES), F32), pltpu.VMEM((nrows, LANES), F32),
                        pltpu.VMEM((nrows, DA_DV), F32)])
    return pl.pallas_call(
        functools.partial(_paged_kernel, n_pages=n_pages), grid_spec=grid_spec,
        out_shape=jax.ShapeDtypeStruct((nb, nrows // 2, DA_DV), F32),
        compiler_params=_params(("parallel", "arbitrary")), name="paged_diff_attn",
    )(page_table, layer, lam2, qmat, *([cache_kt] * npg), *([cache_v2] * npg), knew_t, vnew, nmask, sw)


def _rope_tables(pos):
    half = ROPE_DIM // 2
    inv_freq = ROPE_THETA ** (-jnp.arange(half, dtype=F32) / half)
    ang = pos.astype(F32)[:, None] * inv_freq[None, :]
    cos, sin = jnp.cos(ang), jnp.sin(ang)
    t = pos.shape[0]
    pad = jnp.zeros((t, DA_DH - ROPE_DIM), F32)
    c = jnp.concatenate([cos, cos, jnp.ones_like(pad)], axis=1)
    s1 = jnp.concatenate([-sin, jnp.zeros_like(sin), pad], axis=1)
    s2 = jnp.concatenate([jnp.zeros_like(sin), sin, pad], axis=1)
    rep = LANES // DA_DH
    return tuple(jnp.tile(a, (1, rep)) for a in (c, s1, s2))


def _ssd_consts():
    q = SSD_CHUNK
    tri = jnp.asarray(np.tril(np.ones((q, q), np.float32)), BF16)
    eye = jnp.asarray(np.eye(LANES, dtype=np.float32), BF16)
    expand = np.zeros((LANES, SSD_INNER), np.float32)
    for j in range(SSD_HEADS):
        expand[j, j * SSD_HEADDIM:(j + 1) * SSD_HEADDIM] = 1.0
    return tri, eye, jnp.asarray(expand, BF16)


def _pad_cols(a, width):
    return jnp.pad(a, [(0, 0)] * (a.ndim - 1) + [(0, width - a.shape[-1])])


def _mixer_inputs(x2, wl, rope, bm):
    q_scale = math.log2(math.e) * DA_DH ** -0.5
    z = _mm(x2, wl['w_z'], BF16, bm, 1024)
    xbc = _mm(x2, wl['w_xbc'], BF16, bm, 1024)
    dtr = _mm(x2, wl['w_dt'], F32, bm, LANES)
    q = _mm(x2, wl['w_q'], BF16, bm, 1024, tuple(t * q_scale for t in rope))
    k = _mm(x2, wl['w_k'], F32, bm, 512, rope)
    v = _mm(x2, wl['w_v'], F32, bm, 512)
    g = _mm(x2, wl['w_g'], BF16, bm, 1024)
    return z, xbc, dtr, q, k, v, g


def _post_mixer(x2, y, o, g, mk, mv, wl, bm, tq):
    nb = mk.shape[0]
    x1 = _merge(y, o, g, x2, wl['w_ssd_br'], wl['w_att_br'], wl['w_o'], wl['ln1_g'], wl['ln1_b'], bm)
    cq = _mm(x1, wl['w_cq'], BF16, bm, 1024)
    t = cq.shape[0] // nb
    if t < 16:
        cq3 = jnp.pad(cq.reshape(nb, t, D_MODEL), ((0, 0), (0, 16 - t), (0, 0)))
        co = _cross(cq3, mk, mv, 16)[:, :t].reshape(nb * t, D_MODEL)
    else:
        co = _cross(cq.reshape(nb, t, D_MODEL), mk, mv, tq).reshape(nb * t, D_MODEL)
    x2n = _mm_res_ln(co, wl['w_co'], x1, wl['ln2_g'], wl['ln2_b'], bm)
    return _mlp(x2n, wl['w_up'], wl['w_down'], wl['ln3_g'], wl['ln3_b'], bm, 1024)


def kernel(x_prompt, x_sample, mem_prompt, cache_k, cache_v, state_ssm, state_conv, cache_mem_k, cache_mem_v, page_table, w_in, conv_w, conv_b, dt_bias, a_log, d_skip, ssd_norm_w, w_ssd_br, lambda_q, lambda_k, subln_w, w_att_br, w_o, ln1_g, ln1_b, w_cq, w_ck, w_cv, w_co, ln2_g, ln2_b, w_up, w_down, ln3_g, ln3_b):
    nbp, seq, _ = x_prompt.shape
    nbs, tdec, _ = x_sample.shape
    n_pool = cache_k.shape[1]
    kvw = DA_KV_HEADS * DA_DV

    sp = tuple(int(v) for v in np.cumsum(IN_SPLITS))
    w_in_b = w_in.astype(BF16)
    rope_p = _rope_tables(jnp.arange(seq, dtype=jnp.int32))
    rope_s = _rope_tables(jnp.tile(PAST_LEN + jnp.arange(tdec, dtype=jnp.int32), nbs))
    consts = _ssd_consts()

    cache_kt = jnp.transpose(cache_k, (0, 1, 3, 4, 5, 2)).reshape(DEPTH, n_pool, kvw, PAGE_SIZE)
    cache_v2 = cache_v.reshape(DEPTH, n_pool, PAGE_SIZE * DA_KV_HEADS, DA_DV)

    nrow = DA_KV_HEADS * 2 * tdec * DA_GROUP
    row_t = (np.arange(nrow) % (tdec * DA_GROUP)) // DA_GROUP
    nmask = jnp.asarray(np.where(np.arange(LANES)[None, :] <= row_t[:, None], 0.0, NEG_BIG), F32)

    xp = x_prompt.reshape(nbp * seq, D_MODEL)
    xs = x_sample.reshape(nbs * tdec, D_MODEL)
    mem2 = mem_prompt.reshape(nbp * N_MEM, D_MODEL)
    zeros_h = jnp.zeros((nbp, SSD_INNER, D_STATE), F32)
    zeros_c = jnp.zeros((nbp, 8, CONV_DIM), F32)

    outs = [[] for _ in range(10)]
    for l in range(DEPTH):
        wl = {
            'w_z': w_in_b[l, :, :sp[0]], 'w_xbc': w_in_b[l, :, sp[0]:sp[1]],
            'w_dt': _pad_cols(w_in_b[l, :, sp[1]:sp[2]], LANES),
            'w_q': w_in_b[l, :, sp[2]:sp[3]], 'w_k': w_in_b[l, :, sp[3]:sp[4]],
            'w_v': w_in_b[l, :, sp[4]:sp[5]], 'w_g': w_in_b[l, :, sp[5]:sp[6]],
            'w_ssd_br': w_ssd_br[l].astype(BF16), 'w_att_br': w_att_br[l].astype(BF16),
            'w_o': w_o[l].astype(BF16), 'w_cq': w_cq[l].astype(BF16), 'w_co': w_co[l].astype(BF16),
            'w_up': w_up[l].astype(BF16), 'w_down': w_down[l].astype(BF16),
            'ln1_g': ln1_g[l][None], 'ln1_b': ln1_b[l][None], 'ln2_g': ln2_g[l][None],
            'ln2_b': ln2_b[l][None], 'ln3_g': ln3_g[l][None], 'ln3_b': ln3_b[l][None],
        }
        ssd_p = (conv_w[l], conv_b[l][None], _pad_cols(dt_bias[l][None], LANES),
                 _pad_cols(a_log[l][None], LANES), jnp.repeat(d_skip[l], SSD_HEADDIM)[None],
                 ssd_norm_w[l][None], consts)
        lam_init = 0.8 - 0.6 * math.exp(-0.3 * l)
        lq = lambda_q[l].astype(F32)
        lk = lambda_k[l].astype(F32)
        lam = jnp.exp(jnp.sum(lq[0] * lk[0])) - jnp.exp(jnp.sum(lq[1] * lk[1])) + lam_init
        lam2 = jnp.stack([lam, jnp.asarray(1.0 - lam_init, F32)]).astype(F32)
        sw = subln_w[l][None]

        mk_p = _mm(mem2, w_ck[l].astype(BF16), F32, 512, 1024)
        mv_p = _mm(mem2, w_cv[l].astype(BF16), F32, 512, 1024)
        z, xbc, dtr, q, k, v, g = _mixer_inputs(xp, wl, rope_p, 1024)
        y, h_p = _ssd(xbc.reshape(nbp, seq, CONV_DIM), dtr.reshape(nbp, seq, LANES),
                      z.reshape(nbp, seq, SSD_INNER), zeros_h, zeros_c, *ssd_p, valid=SSD_CHUNK)
        qt = jnp.transpose(q.reshape(nbp, seq, DA_KV_HEADS, DA_GROUP, LANES), (0, 2, 3, 4, 1))
        vt = jnp.transpose(v.astype(BF16).reshape(nbp, seq, DA_KV_HEADS, DA_DV), (0, 2, 3, 1))
        vt = jnp.concatenate([vt, jnp.ones((nbp, DA_KV_HEADS, 16, seq), BF16)], axis=2)
        o_t = _flash(lam2, qt, k.astype(BF16).reshape(nbp, seq, kvw), vt,
                     jnp.broadcast_to(subln_w[l][:, None], (DA_DV, LANES)), FLASH_BQ, FLASH_BK)
        o = jnp.transpose(o_t, (0, 4, 1, 2, 3))
        xp = _post_mixer(xp, y.reshape(nbp * seq, SSD_INNER), o.reshape(nbp * seq, D_MODEL), g,
                         mk_p.astype(BF16).reshape(nbp, N_MEM, D_MODEL),
                         mv_p.astype(BF16).reshape(nbp, N_MEM, D_MODEL), wl, 512, 512)
        c_p = xbc.reshape(nbp, seq, CONV_DIM)[:, seq - (CONV_W - 1):].astype(F32)

        zs, xbcs, dtrs, qs, ks, vs, gs = _mixer_inputs(xs, wl, rope_s, LANES)
        pad_t = lambda a: jnp.pad(a.reshape(nbs, tdec, -1), ((0, 0), (0, SSD_CHUNK - tdec), (0, 0)))
        cprev = jnp.pad(state_conv[l], ((0, 0), (8 - (CONV_W - 1), 0), (0, 0)))
        ys, h_s = _ssd(pad_t(xbcs), pad_t(dtrs), pad_t(zs),
                       state_ssm[l].reshape(nbs, SSD_INNER, D_STATE), cprev, *ssd_p, valid=tdec)
        ys = ys[:, :tdec].reshape(nbs * tdec, SSD_INNER)
        c_s = jnp.concatenate([state_conv[l], xbcs.reshape(nbs, tdec, CONV_DIM).astype(F32)],
                              axis=1)[:, -(CONV_W - 1):]

        q6 = qs.astype(F32).reshape(nbs, tdec, DA_KV_HEADS, DA_GROUP, 2, DA_DH)
        q8 = jnp.transpose(q6, (0, 2, 4, 1, 3, 5)).reshape(nbs, DA_KV_HEADS * 2, tdec * DA_GROUP, DA_DH)
        qmat = jnp.einsum('bktd,kj->bktjd', q8, jnp.eye(DA_KV_HEADS * 2, dtype=F32))
        qmat = qmat.reshape(nbs, nrow, kvw).astype(BF16)
        knew_t = _pad_cols(jnp.transpose(ks.reshape(nbs, tdec, kvw), (0, 2, 1)), LANES)
        vnew = jnp.pad(vs.reshape(nbs, tdec, kvw), ((0, 0), (0, LANES - tdec), (0, 0)))
        o_s = _paged(page_table, jnp.full((1,), l, jnp.int32), lam2, qmat,
                     cache_kt, cache_v2, knew_t, vnew, nmask, sw)
        o_s = jnp.transpose(o_s.reshape(nbs, DA_KV_HEADS, tdec, DA_GROUP, DA_DV), (0, 2, 1, 3, 4))
        o_s = o_s.reshape(nbs * tdec, D_MODEL).astype(BF16)
        xs = _post_mixer(xs, ys, o_s, gs, cache_mem_k[l].astype(BF16).reshape(nbs, N_MEM, D_MODEL),
                         cache_mem_v[l].astype(BF16).reshape(nbs, N_MEM, D_MODEL), wl, LANES, 16)

        vals = (k.reshape(nbp, seq, DA_KV_HEADS, 2, DA_DH), v.reshape(nbp, seq, DA_KV_HEADS, DA_DV),
                h_p.reshape(nbp, SSD_HEADS, SSD_HEADDIM, D_STATE), c_p,
                mk_p.reshape(nbp, N_MEM, CA_HEADS, CA_DH), mv_p.reshape(nbp, N_MEM, CA_HEADS, CA_DH),
                ks.reshape(nbs, tdec, DA_KV_HEADS, 2, DA_DH), vs.reshape(nbs, tdec, DA_KV_HEADS, DA_DV),
                h_s.reshape(nbs, SSD_HEADS, SSD_HEADDIM, D_STATE), c_s)
        for lst, val in zip(outs, vals):
            lst.append(val)

    return (xp.reshape(nbp, seq, D_MODEL), xs.reshape(nbs, tdec, D_MODEL)) + tuple(jnp.stack(o) for o in outs)
```

```python
import functools
import math

import jax
import jax.numpy as jnp
import numpy as np
from jax import lax
from jax.experimental import pallas as pl
from jax.experimental.pallas import tpu as pltpu

F32 = jnp.float32
BF16 = jnp.bfloat16

D_MODEL = 1024
DEPTH = 4
PAST_LEN = 8192
PAGE_SIZE = 128
SSD_INNER = 2 * D_MODEL
SSD_HEADDIM = 64
SSD_HEADS = SSD_INNER // SSD_HEADDIM
SSD_GROUPS = 4
SSD_HPG = SSD_HEADS // SSD_GROUPS
D_STATE = 128
CONV_W = 4
CONV_DIM = SSD_INNER + 2 * SSD_GROUPS * D_STATE
SSD_CHUNK = 128
SAMPLE_CHUNK = 16
DA_HEADS = 8
DA_KV_HEADS = 4
DA_GROUP = DA_HEADS // DA_KV_HEADS
DA_DH = 64
DA_DV = 2 * DA_DH
ROPE_DIM = DA_DH // 4
ROPE_THETA = 500000.0
N_MEM = 256
CA_HEADS = 4
CA_DH = D_MODEL // CA_HEADS
D_FF = 4 * D_MODEL
IN_SPLITS = (SSD_INNER, CONV_DIM, SSD_HEADS, DA_HEADS * 2 * DA_DH,
             DA_KV_HEADS * 2 * DA_DH, DA_KV_HEADS * DA_DV, 2 * D_MODEL)
DN_ALPHA = (2 * DEPTH) ** 0.25
LN_EPS = 1e-5
RMS_EPS = 1e-6

LANES = 128
NEG_BIG = -1e30
VMEM_LIMIT = 56 * 1024 * 1024
PAGES_PER_STEP = 16
FLASH_BQ = 1024
FLASH_BK = 1024
ONES_ROWS = 16
Q_SCALE = math.log2(math.e) * DA_DH ** -0.5

NT_DIMS = (((1,), (1,)), ((), ()))
TN_DIMS = (((0,), (0,)), ((), ()))


def _params(sem):
    return pltpu.CompilerParams(dimension_semantics=sem, vmem_limit_bytes=VMEM_LIMIT)


def _dot(a, b):
    return jnp.dot(a, b, preferred_element_type=F32)


def _dot_nt(a, b):
    return lax.dot_general(a, b, NT_DIMS, preferred_element_type=F32)


def _dot_tn(a, b):
    return lax.dot_general(a, b, TN_DIMS, preferred_element_type=F32)


def _split3(a):
    hi = a.astype(BF16)
    r = a - hi.astype(F32)
    mid = r.astype(BF16)
    lo = (r - mid.astype(F32)).astype(BF16)
    return hi, mid, lo


def _sigmoid(x):
    return 1.0 / (1.0 + jnp.exp(-x))


def _layer_norm(v, g, b):
    mu = jnp.mean(v, axis=-1, keepdims=True)
    d = v - mu
    var = jnp.mean(d * d, axis=-1, keepdims=True)
    return d * lax.rsqrt(var + LN_EPS) * g + b


def _mm_kernel(x_ref, w_ref, o_ref):
    o_ref[...] = _dot(x_ref[...].astype(BF16), w_ref[...]).astype(o_ref.dtype)


def _mm_rope_kernel(x_ref, w_ref, c_ref, s1_ref, s2_ref, o_ref):
    acc = _dot(x_ref[...].astype(BF16), w_ref[...])
    c, s1, s2 = c_ref[...], s1_ref[...], s2_ref[...]
    for cb in range(acc.shape[1] // LANES):
        blk = acc[:, cb * LANES:(cb + 1) * LANES]
        up = pltpu.roll(blk, LANES - ROPE_DIM // 2, 1)
        dn = pltpu.roll(blk, ROPE_DIM // 2, 1)
        o_ref[:, cb * LANES:(cb + 1) * LANES] = (blk * c + up * s1 + dn * s2).astype(o_ref.dtype)


def _mm(x, w, out_dtype, bm, bn, rope=None):
    m, k = x.shape
    n = w.shape[1]
    bm, bn = min(bm, m), min(bn, n)
    if rope is not None:
        bm = min(bm, rope[0].shape[0])
    in_specs = [pl.BlockSpec((bm, k), lambda i, j: (i, 0)),
                pl.BlockSpec((k, bn), lambda i, j: (0, j))]
    args = [x, w]
    body = _mm_kernel
    if rope is not None:
        nrep = rope[0].shape[0] // bm
        for t in rope:
            in_specs.append(pl.BlockSpec((bm, LANES), lambda i, j: (i % nrep, 0)))
            args.append(t)
        body = _mm_rope_kernel
    return pl.pallas_call(
        body, grid=(m // bm, n // bn), in_specs=in_specs,
        out_specs=pl.BlockSpec((bm, bn), lambda i, j: (i, j)),
        out_shape=jax.ShapeDtypeStruct((m, n), out_dtype),
        compiler_params=_params(("parallel", "parallel")), name="proj",
    )(*args)


def _qkv_kernel(x_ref, w_ref, c_ref, s1_ref, s2_ref, qt_ref, k_ref, kb_ref, v_ref, vt_ref, *, q_scale):
    acc = _dot(x_ref[...].astype(BF16), w_ref[...])
    c, s1, s2 = c_ref[...], s1_ref[...], s2_ref[...]
    nq, nk = DA_HEADS * 2 * DA_DH // LANES, DA_KV_HEADS * 2 * DA_DH // LANES

    def rope(cb):
        blk = acc[:, cb * LANES:(cb + 1) * LANES]
        up = pltpu.roll(blk, LANES - ROPE_DIM // 2, 1)
        dn = pltpu.roll(blk, ROPE_DIM // 2, 1)
        return blk * c + up * s1 + dn * s2

    for cb in range(nq):
        qt_ref[0, cb * LANES:(cb + 1) * LANES, :] = (rope(cb) * q_scale).T.astype(BF16)
    for cb in range(nk):
        kblk = rope(nq + cb)
        k_ref[:, cb * LANES:(cb + 1) * LANES] = kblk
        kb_ref[:, cb * LANES:(cb + 1) * LANES] = kblk.astype(BF16)
    ones = jnp.ones((ONES_ROWS, acc.shape[0]), BF16)
    for h in range(DA_KV_HEADS):
        vblk = acc[:, (nq + nk + h) * LANES:(nq + nk + h + 1) * LANES]
        v_ref[:, h * DA_DV:(h + 1) * DA_DV] = vblk
        vt_ref[0, h, :DA_DV, :] = vblk.T.astype(BF16)
        vt_ref[0, h, DA_DV:, :] = ones


def _qkv(x, w_qkv, rope, nb, bm, q_scale):
    m, kdim = x.shape
    seq = m // nb
    bm = min(bm, seq)
    per_seq = seq // bm
    qw, kw = DA_HEADS * 2 * DA_DH, DA_KV_HEADS * 2 * DA_DH
    vw = DA_KV_HEADS * DA_DV
    row = lambda w: pl.BlockSpec((bm, w), lambda i: (i, 0))
    tab = pl.BlockSpec((bm, LANES), lambda i: (i % per_seq, 0))
    return pl.pallas_call(
        functools.partial(_qkv_kernel, q_scale=q_scale), grid=(m // bm,),
        in_specs=[row(kdim), pl.BlockSpec(w_qkv.shape, lambda i: (0, 0)), tab, tab, tab],
        out_specs=[pl.BlockSpec((1, qw, bm), lambda i: (i // per_seq, 0, i % per_seq)),
                   row(kw), row(kw), row(vw),
                   pl.BlockSpec((1, DA_KV_HEADS, DA_DV + ONES_ROWS, bm),
                                lambda i: (i // per_seq, 0, 0, i % per_seq))],
        out_shape=[jax.ShapeDtypeStruct((nb, qw, seq), BF16),
                   jax.ShapeDtypeStruct((m, kw), F32), jax.ShapeDtypeStruct((m, kw), BF16),
                   jax.ShapeDtypeStruct((m, vw), F32),
                   jax.ShapeDtypeStruct((nb, DA_KV_HEADS, DA_DV + ONES_ROWS, seq), BF16)],
        compiler_params=_params(("parallel",)), name="qkv_proj",
    )(x, w_qkv, *rope)


def _mm_res_ln_kernel(a_ref, w_ref, x_ref, g_ref, b_ref, o_ref):
    y = _dot(a_ref[...], w_ref[...])
    o_ref[...] = _layer_norm(DN_ALPHA * x_ref[...] + y, g_ref[...], b_ref[...])


def _mm_res_ln(a, w, x, g, b, bm):
    m, k = a.shape
    n = w.shape[1]
    bm = min(bm, m)
    return pl.pallas_call(
        _mm_res_ln_kernel, grid=(m // bm,),
        in_specs=[pl.BlockSpec((bm, k), lambda i: (i, 0)),
                  pl.BlockSpec((k, n), lambda i: (0, 0)),
                  pl.BlockSpec((bm, n), lambda i: (i, 0)),
                  pl.BlockSpec((1, n), lambda i: (0, 0)),
                  pl.BlockSpec((1, n), lambda i: (0, 0))],
        out_specs=pl.BlockSpec((bm, n), lambda i: (i, 0)),
        out_shape=jax.ShapeDtypeStruct((m, n), F32),
        compiler_params=_params(("parallel",)), name="proj_res_ln",
    )(a, w, x, g, b)


def _merge_kernel(y_ref, o_ref, g_ref, x_ref, wbr_ref, watt_ref, wo_ref, lg_ref, lb_ref, out_ref,
                  *, o_transposed):
    ssd = _dot(y_ref[...], wbr_ref[...])
    if o_transposed:
        att = _dot_tn(o_ref[0], watt_ref[...])
    else:
        att = _dot(o_ref[...], watt_ref[...])
    g = g_ref[...].astype(F32)
    merged = _sigmoid(g[:, :D_MODEL]) * ssd + _sigmoid(g[:, D_MODEL:]) * att
    mix = _dot(merged.astype(BF16), wo_ref[...])
    out_ref[...] = _layer_norm(DN_ALPHA * x_ref[...] + mix, lg_ref[...], lb_ref[...])


def _merge(y, o, g, x, wbr, watt, wo, lg, lb, bm):
    m = x.shape[0]
    bm = min(bm, m)
    row = lambda w: pl.BlockSpec((bm, w), lambda i: (i, 0))
    full = lambda a: pl.BlockSpec(a.shape, lambda i: (0, 0))
    o_transposed = o.ndim == 3
    if o_transposed:
        per_seq = o.shape[2] // bm
        o_spec = pl.BlockSpec((1, D_MODEL, bm), lambda i: (i // per_seq, 0, i % per_seq))
    else:
        o_spec = row(D_MODEL)
    return pl.pallas_call(
        functools.partial(_merge_kernel, o_transposed=o_transposed), grid=(m // bm,),
        in_specs=[row(SSD_INNER), o_spec, row(2 * D_MODEL), row(D_MODEL),
                  full(wbr), full(watt), full(wo), full(lg), full(lb)],
        out_specs=row(D_MODEL),
        out_shape=jax.ShapeDtypeStruct((m, D_MODEL), F32),
        compiler_params=_params(("parallel",)), name="merge",
    )(y, o, g, x, wbr, watt, wo, lg, lb)


def _mlp_kernel(x_ref, wu_ref, wd_ref, lg_ref, lb_ref, o_ref, acc_sc, xb_sc):
    j = pl.program_id(1)

    @pl.when(j == 0)
    def _():
        acc_sc[...] = jnp.zeros_like(acc_sc)
        xb_sc[...] = x_ref[...].astype(BF16)

    h = jnp.maximum(_dot(xb_sc[...], wu_ref[...]), 0.0)
    acc_sc[...] += _dot((h * h).astype(BF16), wd_ref[...])

    @pl.when(j == pl.num_programs(1) - 1)
    def _():
        o_ref[...] = _layer_norm(DN_ALPHA * x_ref[...] + acc_sc[...], lg_ref[...], lb_ref[...])


def _mlp(x, wu, wd, lg, lb, bm, bf):
    m = x.shape[0]
    bm = min(bm, m)
    return pl.pallas_call(
        _mlp_kernel, grid=(m // bm, D_FF // bf),
        in_specs=[pl.BlockSpec((bm, D_MODEL), lambda i, j: (i, 0)),
                  pl.BlockSpec((D_MODEL, bf), lambda i, j: (0, j)),
                  pl.BlockSpec((bf, D_MODEL), lambda i, j: (j, 0)),
                  pl.BlockSpec((1, D_MODEL), lambda i, j: (0, 0)),
                  pl.BlockSpec((1, D_MODEL), lambda i, j: (0, 0))],
        out_specs=pl.BlockSpec((bm, D_MODEL), lambda i, j: (i, 0)),
        out_shape=jax.ShapeDtypeStruct((m, D_MODEL), F32),
        scratch_shapes=[pltpu.VMEM((bm, D_MODEL), F32), pltpu.VMEM((bm, D_MODEL), BF16)],
        compiler_params=_params(("parallel", "arbitrary")), name="mlp",
    )(x, wu, wd, lg, lb)


def _cross_kernel(q_ref, k_ref, v_ref, o_ref):
    q = q_ref[0]
    k = k_ref[0]
    v = v_ref[0]
    for h in range(CA_HEADS):
        sl = slice(h * CA_DH, (h + 1) * CA_DH)
        s = _dot_nt(q[:, sl], k[:, sl]) * (CA_DH ** -0.5)
        s = s - jnp.max(s, axis=-1, keepdims=True)
        p = jnp.exp(s)
        p = p / jnp.sum(p, axis=-1, keepdims=True)
        o_ref[0, :, sl] = _dot(p.astype(BF16), v[:, sl]).astype(o_ref.dtype)


def _cross(q, mk, mv, tq):
    b, t, d = q.shape
    tq = min(tq, t)
    return pl.pallas_call(
        _cross_kernel, grid=(b, t // tq),
        in_specs=[pl.BlockSpec((1, tq, d), lambda i, j: (i, j, 0)),
                  pl.BlockSpec((1, N_MEM, d), lambda i, j: (i, 0, 0)),
                  pl.BlockSpec((1, N_MEM, d), lambda i, j: (i, 0, 0))],
        out_specs=pl.BlockSpec((1, tq, d), lambda i, j: (i, j, 0)),
        out_shape=jax.ShapeDtypeStruct((b, t, d), BF16),
        compiler_params=_params(("parallel", "parallel")), name="cross_attn",
    )(q, mk, mv)


def _ssd_kernel(xbc_ref, dt_ref, z_ref, h0_ref, cprev_ref, cw_ref, cb_ref, dtb_ref, alog_ref,
                dskip_ref, nw_ref, tri_ref, eye_ref, exp_ref, y_ref, hout_ref,
                h_sc, xpad_sc, y_sc, *, q_len, valid):
    c = pl.program_id(1)

    @pl.when(c == 0)
    def _():
        h_sc[...] = h0_ref[0]
        xpad_sc[0:8, :] = cprev_ref[0]

    xpad_sc[8:8 + q_len, :] = xbc_ref[0].astype(F32)
    cw = cw_ref[...]
    conv = cb_ref[...] + cw[3:4, :] * xpad_sc[8:8 + q_len, :]
    for kk in range(1, CONV_W):
        conv = conv + cw[3 - kk:4 - kk, :] * xpad_sc[8 - kk:8 - kk + q_len, :]
    xpad_sc[0:8, :] = xpad_sc[q_len:q_len + 8, :]
    xbc = conv * _sigmoid(conv)
    xs = xbc[:, :SSD_INNER]

    dtr = dt_ref[0] + dtb_ref[...]
    dtv = jnp.maximum(dtr, 0.0) + jnp.log(1.0 + jnp.exp(-jnp.abs(dtr)))
    if valid < q_len:
        rows = lax.broadcasted_iota(jnp.int32, dtv.shape, 0)
        dtv = jnp.where(rows < valid, dtv, 0.0)
    a = dtv * (-jnp.exp(alog_ref[...]))

    tri = tri_ref[...]
    eye = eye_ref[...]
    expand = exp_ref[...]
    a_parts = _split3(a)
    acum = sum(_dot(tri, p) for p in a_parts)
    a_t = sum(_dot_nt(eye, p) for p in a_parts)
    acum_t = sum(_dot_nt(p, tri) for p in _split3(a_t))
    acum_x = sum(_dot(p, expand) for p in _split3(acum))
    dt_x = sum(_dot(p, expand) for p in _split3(dtv))

    xdt = xs * dt_x
    xdt_b = xdt.astype(BF16)
    xdec = (xdt * jnp.exp(acum_x[q_len - 1:q_len, :] - acum_x)).astype(BF16)
    eacum_x = jnp.exp(acum_x)
    chunk_decay = jnp.exp(jnp.broadcast_to(acum_t[:, q_len - 1:q_len], (LANES, LANES)))

    li = lax.broadcasted_iota(jnp.int32, (q_len, q_len), 0)
    si = lax.broadcasted_iota(jnp.int32, (q_len, q_len), 1)
    causal = li >= si
    left = lax.broadcasted_iota(jnp.int32, (q_len, LANES), 1) < SSD_HEADDIM
    gw = SSD_HPG * SSD_HEADDIM
    for g in range(SSD_GROUPS):
        b_g = xbc[:, SSD_INNER + g * D_STATE:SSD_INNER + (g + 1) * D_STATE].astype(BF16)
        c_off = SSD_INNER + SSD_GROUPS * D_STATE
        c_g = xbc[:, c_off + g * D_STATE:c_off + (g + 1) * D_STATE].astype(BF16)
        cbm = _dot_nt(c_g, b_g)
        h_g = h_sc[g * gw:(g + 1) * gw, :]
        y_off = _dot_nt(c_g, h_g.astype(BF16))
        for pp in range(SSD_HPG // 2):
            j0 = g * SSD_HPG + 2 * pp
            cols = slice(j0 * SSD_HEADDIM, (j0 + 2) * SSD_HEADDIM)
            x_pair = xdt_b[:, cols]
            halves = []
            for j in (j0, j0 + 1):
                seg = acum[:, j:j + 1] - acum_t[j:j + 1, :]
                m_j = (cbm * jnp.where(causal, jnp.exp(seg), 0.0)).astype(BF16)
                halves.append(_dot(m_j, x_pair))
            y_diag = jnp.where(left, halves[0], halves[1])
            y_sc[:, cols] = (y_diag + y_off[:, pp * LANES:(pp + 1) * LANES] * eacum_x[:, cols]
                             + xs[:, cols] * dskip_ref[:, cols])
        new_states = _dot_tn(xdec[:, g * gw:(g + 1) * gw], b_g)
        for jj in range(SSD_HPG):
            j = g * SSD_HPG + jj
            r = slice(j * SSD_HEADDIM, (j + 1) * SSD_HEADDIM)
            rl = slice(jj * SSD_HEADDIM, (jj + 1) * SSD_HEADDIM)
            h_sc[r, :] = h_g[rl, :] * chunk_decay[j:j + 1, :] + new_states[rl, :]

    z = z_ref[0].astype(F32)
    yz = y_sc[...] * (z * _sigmoid(z))
    ms = jnp.mean(yz * yz, axis=-1, keepdims=True)
    y_ref[0] = (yz * lax.rsqrt(ms + RMS_EPS) * nw_ref[...]).astype(y_ref.dtype)

    @pl.when(c == pl.num_programs(1) - 1)
    def _():
        hout_ref[0] = h_sc[...]


def _ssd(xbc, dtr, z, h0, cprev, cw, cb, dtb, alog, dskip_x, nw, consts, valid):
    b, l, _ = xbc.shape
    tri, eye, expand = consts
    q_len = tri.shape[0]
    nc = l // q_len
    tok = lambda w: pl.BlockSpec((1, q_len, w), lambda i, c: (i, c, 0))
    per_b = lambda r, w: pl.BlockSpec((1, r, w), lambda i, c: (i, 0, 0))
    full = lambda a: pl.BlockSpec(a.shape, lambda i, c: (0,) * a.ndim)
    return pl.pallas_call(
        functools.partial(_ssd_kernel, q_len=q_len, valid=valid),
        grid=(b, nc),
        in_specs=[tok(CONV_DIM), tok(LANES), tok(SSD_INNER),
                  per_b(SSD_INNER, D_STATE), per_b(8, CONV_DIM),
                  full(cw), full(cb), full(dtb), full(alog), full(dskip_x), full(nw),
                  full(tri), full(eye), full(expand)],
        out_specs=[tok(SSD_INNER), per_b(SSD_INNER, D_STATE)],
        out_shape=[jax.ShapeDtypeStruct((b, l, SSD_INNER), BF16),
                   jax.ShapeDtypeStruct((b, SSD_INNER, D_STATE), F32)],
        scratch_shapes=[pltpu.VMEM((SSD_INNER, D_STATE), F32),
                        pltpu.VMEM((q_len + 8, CONV_DIM), F32),
                        pltpu.VMEM((q_len, SSD_INNER), F32)],
        compiler_params=_params(("parallel", "arbitrary")), name="ssd",
    )(xbc, dtr, z, h0, cprev, cw, cb, dtb, alog, dskip_x, nw, tri, eye, expand)


def _flash_kernel(qm_ref, km_ref, lam_ref, qt_ref, k_ref, vt_ref, sw_ref, o_ref,
                  qw_sc, m_sc, acc_sc, *, bq, bk):
    step = pl.program_id(2)
    qi = qm_ref[step]
    kj = km_ref[step]
    cols = DA_GROUP * bq

    @pl.when(kj == 0)
    def _():
        row = lax.broadcasted_iota(jnp.int32, (LANES, bq), 0)
        for g in range(DA_GROUP):
            qg = qt_ref[0, 0, g].astype(F32)
            qw_sc[0, :, g * bq:(g + 1) * bq] = jnp.where(row < DA_DH, qg, 0.0).astype(BF16)
            qw_sc[1, :, g * bq:(g + 1) * bq] = jnp.where(row >= DA_DH, qg, 0.0).astype(BF16)
        m_sc[...] = jnp.full(m_sc.shape, NEG_BIG, F32)
        acc_sc[...] = jnp.zeros_like(acc_sc)

    def update(masked):
        k = k_ref[0]
        vt = vt_ref[0, 0]
        if masked:
            kpos = kj * bk + lax.broadcasted_iota(jnp.int32, (bk, cols), 0)
            c = lax.broadcasted_iota(jnp.int32, (bk, cols), 1)
            mask = qi * bq + c % bq >= kpos
        for i in range(2):
            st = _dot(k, qw_sc[i])
            if masked:
                st = jnp.where(mask, st, NEG_BIG)
            m_prev = m_sc[i]
            m_new = jnp.maximum(m_prev, jnp.max(st, axis=0, keepdims=True))
            p = jnp.exp2(st - m_new).astype(BF16)
            acc_sc[i] = jnp.exp2(m_prev - m_new) * acc_sc[i] + _dot(vt, p)
            m_sc[i] = m_new

    on_diag = kj * bk + bk - 1 > qi * bq
    pl.when(on_diag)(functools.partial(update, True))
    pl.when(jnp.logical_not(on_diag))(functools.partial(update, False))

    @pl.when(kj == (qi + 1) * (bq // bk) - 1)
    def _():
        lam = lam_ref[0]
        post = lam_ref[1]
        sw = jnp.concatenate([sw_ref[...]] * (bq // LANES), axis=1)
        for g in range(DA_GROUP):
            cs = slice(g * bq, (g + 1) * bq)
            o = (acc_sc[0, :DA_DV, cs] / acc_sc[0, DA_DV:DA_DV + 1, cs]
                 - lam * (acc_sc[1, :DA_DV, cs] / acc_sc[1, DA_DV:DA_DV + 1, cs]))
            ms = jnp.mean(o * o, axis=0, keepdims=True)
            o_ref[0, 0, g] = (o * lax.rsqrt(ms + RMS_EPS) * sw * post).astype(o_ref.dtype)


def _flash(lam2, qt, k, vt, sw_b, bq, bk):
    b, s, _ = k.shape
    dva = vt.shape[2]
    bq, bk = min(bq, s), min(bk, s)
    assert bq % bk == 0 and s % bq == 0 and bq % LANES == 0
    ratio = bq // bk
    pairs = [(qi, kj) for qi in range(s // bq) for kj in range((qi + 1) * ratio)]
    qmap = jnp.asarray([p[0] for p in pairs], jnp.int32)
    kmap = jnp.asarray([p[1] for p in pairs], jnp.int32)
    cols = DA_GROUP * bq
    grid_spec = pltpu.PrefetchScalarGridSpec(
        num_scalar_prefetch=2, grid=(b, DA_KV_HEADS, len(pairs)),
        in_specs=[pl.BlockSpec(memory_space=pltpu.SMEM),
                  pl.BlockSpec((1, 1, DA_GROUP, LANES, bq), lambda bi, h, st, qm, km: (bi, h, 0, 0, qm[st])),
                  pl.BlockSpec((1, bk, LANES), lambda bi, h, st, qm, km: (bi, km[st], h)),
                  pl.BlockSpec((1, 1, dva, bk), lambda bi, h, st, qm, km: (bi, h, 0, km[st])),
                  pl.BlockSpec((DA_DV, LANES), lambda bi, h, st, qm, km: (0, 0))],
        out_specs=pl.BlockSpec((1, 1, DA_GROUP, DA_DV, bq), lambda bi, h, st, qm, km: (bi, h, 0, 0, qm[st])),
        scratch_shapes=[pltpu.VMEM((2, LANES, cols), BF16),
                        pltpu.VMEM((2, 1, cols), F32),
                        pltpu.VMEM((2, dva, cols), F32)])
    return pl.pallas_call(
        functools.partial(_flash_kernel, bq=bq, bk=bk), grid_spec=grid_spec,
        out_shape=jax.ShapeDtypeStruct((b, DA_KV_HEADS, DA_GROUP, DA_DV, s), BF16),
        compiler_params=_params(("parallel", "parallel", "arbitrary")), name="flash_diff_attn",
    )(qmap, kmap, lam2, qt, k, vt, sw_b)


def _paged_kernel(pt_ref, lay_ref, lam_ref, q_ref, *rest, n_pages):
    npg = PAGES_PER_STEP
    kp, vp = rest[:npg], rest[npg:2 * npg]
    knew_ref, vnew_ref, nmask_ref, sw_ref, o_ref, m_sc, l_sc, acc_sc = rest[2 * npg:]
    step = pl.program_id(1)
    n_steps = n_pages // npg
    hrows = q_ref.shape[1] // DA_KV_HEADS

    @pl.when(step == 0)
    def _():
        m_sc[...] = jnp.full(m_sc.shape, NEG_BIG, F32)
        l_sc[...] = jnp.zeros_like(l_sc)
        acc_sc[...] = jnp.zeros_like(acc_sc)

    def update(s, v_of_head):
        m_prev = m_sc[...]
        m_new = jnp.maximum(m_prev, jnp.max(s, axis=-1, keepdims=True))
        alpha = jnp.exp2(m_prev - m_new)
        p = jnp.exp2(s - jnp.concatenate([m_new] * (s.shape[1] // LANES), axis=1))
        l_sc[...] = alpha * l_sc[...] + jnp.sum(p, axis=-1, keepdims=True)
        pb = p.astype(BF16)
        for h in range(DA_KV_HEADS):
            rs = slice(h * hrows, (h + 1) * hrows)
            acc_sc[rs, :] = alpha[rs, :] * acc_sc[rs, :] + _dot(pb[rs, :], v_of_head(h))
        m_sc[...] = m_new

    @pl.when(step < n_steps)
    def _():
        kt = jnp.concatenate([r[0, 0].astype(BF16) for r in kp], axis=1)
        s = _dot(q_ref[0], kt)

        def v_of_head(h):
            return jnp.concatenate(
                [r[0, 0, pl.ds(h, PAGE_SIZE, stride=DA_KV_HEADS), :] for r in vp], axis=0).astype(BF16)

        update(s, v_of_head)

    @pl.when(step == n_steps)
    def _():
        s = _dot(q_ref[0], knew_ref[0].astype(BF16)) + nmask_ref[...]
        update(s, lambda h: vnew_ref[0][:, h * DA_DV:(h + 1) * DA_DV].astype(BF16))
        lam = lam_ref[0]
        post = lam_ref[1]
        half = hrows // 2
        for h in range(DA_KV_HEADS):
            r0 = slice(h * hrows, h * hrows + half)
            r1 = slice(h * hrows + half, (h + 1) * hrows)
            o = acc_sc[r0, :] / l_sc[r0, :] - lam * (acc_sc[r1, :] / l_sc[r1, :])
            ms = jnp.mean(o * o, axis=-1, keepdims=True)
            o_ref[0, h * half:(h + 1) * half, :] = o * lax.rsqrt(ms + RMS_EPS) * sw_ref[...] * post


def _paged(page_table, layer, lam2, qmat, cache_kt, cache_v2, knew_t, vnew, nmask, sw):
    nb, n_pages = page_table.shape
    npg = PAGES_PER_STEP
    assert n_pages % npg == 0
    n_steps = n_pages // npg
    nrows = qmat.shape[1]
    kvw = cache_kt.shape[2]

    def page_spec(o):
        def imap(b, s, pt, lay):
            return (lay[0], pt[b, jnp.minimum(s, n_steps - 1) * npg + o], 0, 0)
        return pl.BlockSpec((1, 1, kvw, LANES), imap)

    per_b = lambda a: pl.BlockSpec((1,) + a.shape[1:], lambda b, s, pt, lay: (b, 0, 0))
    full = lambda a: pl.BlockSpec(a.shape, lambda b, s, pt, lay: (0, 0))
    in_specs = ([pl.BlockSpec(memory_space=pltpu.SMEM), per_b(qmat)]
                + [page_spec(o) for o in range(npg)] + [page_spec(o) for o in range(npg)]
                + [per_b(knew_t), per_b(vnew), full(nmask), full(sw)])
    grid_spec = pltpu.PrefetchScalarGridSpec(
        num_scalar_prefetch=2, grid=(nb, n_steps + 1), in_specs=in_specs,
        out_specs=pl.BlockSpec((1, nrows // 2, DA_DV), lambda b, s, pt, lay: (b, 0, 0)),
        scratch_shapes=[pltpu.VMEM((nrows, LANES), F32), pltpu.VMEM((nrows, LANES), F32),
                        pltpu.VMEM((nrows, DA_DV), F32)])
    return pl.pallas_call(
        functools.partial(_paged_kernel, n_pages=n_pages), grid_spec=grid_spec,
        out_shape=jax.ShapeDtypeStruct((nb, nrows // 2, DA_DV), F32),
        compiler_params=_params(("parallel", "arbitrary")), name="paged_diff_attn",
    )(page_table, layer, lam2, qmat, *([cache_kt] * npg), *([cache_v2] * npg), knew_t, vnew, nmask, sw)


def _rope_tables(pos):
    half = ROPE_DIM // 2
    inv_freq = ROPE_THETA ** (-jnp.arange(half, dtype=F32) / half)
    ang = pos.astype(F32)[:, None] * inv_freq[None, :]
    cos, sin = jnp.cos(ang), jnp.sin(ang)
    t = pos.shape[0]
    pad = jnp.zeros((t, DA_DH - ROPE_DIM), F32)
    c = jnp.concatenate([cos, cos, jnp.ones_like(pad)], axis=1)
    s1 = jnp.concatenate([-sin, jnp.zeros_like(sin), pad], axis=1)
    s2 = jnp.concatenate([jnp.zeros_like(sin), sin, pad], axis=1)
    rep = LANES // DA_DH
    return tuple(jnp.tile(a, (1, rep)) for a in (c, s1, s2))


def _ssd_consts(q):
    tri = jnp.asarray(np.tril(np.ones((q, q), np.float32)), BF16)
    eye = jnp.asarray(np.eye(LANES, dtype=np.float32), BF16)
    expand = np.zeros((LANES, SSD_INNER), np.float32)
    for j in range(SSD_HEADS):
        expand[j, j * SSD_HEADDIM:(j + 1) * SSD_HEADDIM] = 1.0
    return tri, eye, jnp.asarray(expand, BF16)


def _pad_cols(a, width):
    return jnp.pad(a, [(0, 0)] * (a.ndim - 1) + [(0, width - a.shape[-1])])


def _ssd_gate_inputs(x2, wl, bm):
    z = _mm(x2, wl['w_z'], BF16, bm, 1024)
    xbc = _mm(x2, wl['w_xbc'], BF16, bm, 1024)
    dtr = _mm(x2, wl['w_dt'], F32, bm, LANES)
    g = _mm(x2, wl['w_g'], BF16, bm, 1024)
    return z, xbc, dtr, g


def _sample_qkv(x2, wl, rope, bm):
    q = _mm(x2, wl['w_q'], BF16, bm, 1024, tuple(t * Q_SCALE for t in rope))
    k = _mm(x2, wl['w_k'], F32, bm, 512, rope)
    v = _mm(x2, wl['w_v'], F32, bm, 512)
    return q, k, v


def _post_mixer(x2, y, o, g, mk, mv, wl, bm, tq):
    nb = mk.shape[0]
    x1 = _merge(y, o, g, x2, wl['w_ssd_br'], wl['w_att_br'], wl['w_o'], wl['ln1_g'], wl['ln1_b'], bm)
    cq = _mm(x1, wl['w_cq'], BF16, bm, 1024)
    t = cq.shape[0] // nb
    if t < 16:
        cq3 = jnp.pad(cq.reshape(nb, t, D_MODEL), ((0, 0), (0, 16 - t), (0, 0)))
        co = _cross(cq3, mk, mv, 16)[:, :t].reshape(nb * t, D_MODEL)
    else:
        co = _cross(cq.reshape(nb, t, D_MODEL), mk, mv, tq).reshape(nb * t, D_MODEL)
    x2n = _mm_res_ln(co, wl['w_co'], x1, wl['ln2_g'], wl['ln2_b'], bm)
    return _mlp(x2n, wl['w_up'], wl['w_down'], wl['ln3_g'], wl['ln3_b'], bm, 1024)


def kernel(x_prompt, x_sample, mem_prompt, cache_k, cache_v, state_ssm, state_conv, cache_mem_k, cache_mem_v, page_table, w_in, conv_w, conv_b, dt_bias, a_log, d_skip, ssd_norm_w, w_ssd_br, lambda_q, lambda_k, subln_w, w_att_br, w_o, ln1_g, ln1_b, w_cq, w_ck, w_cv, w_co, ln2_g, ln2_b, w_up, w_down, ln3_g, ln3_b):
    nbp, seq, _ = x_prompt.shape
    nbs, tdec, _ = x_sample.shape
    n_pool = cache_k.shape[1]
    kvw = DA_KV_HEADS * DA_DV

    sp = tuple(int(v) for v in np.cumsum(IN_SPLITS))
    w_in_b = w_in.astype(BF16)
    rope_p = _rope_tables(jnp.arange(seq, dtype=jnp.int32))
    rope_s = _rope_tables(jnp.tile(PAST_LEN + jnp.arange(tdec, dtype=jnp.int32), nbs))
    consts_p = _ssd_consts(SSD_CHUNK)
    consts_s = _ssd_consts(SAMPLE_CHUNK)

    cache_kt = jnp.transpose(cache_k, (0, 1, 3, 4, 5, 2)).reshape(DEPTH, n_pool, kvw, PAGE_SIZE)
    cache_v2 = cache_v.reshape(DEPTH, n_pool, PAGE_SIZE * DA_KV_HEADS, DA_DV)

    nrow = DA_KV_HEADS * 2 * tdec * DA_GROUP
    row_t = (np.arange(nrow) % (tdec * DA_GROUP)) // DA_GROUP
    nmask = jnp.asarray(np.where(np.arange(LANES)[None, :] <= row_t[:, None], 0.0, NEG_BIG), F32)

    xp = x_prompt.reshape(nbp * seq, D_MODEL)
    xs = x_sample.reshape(nbs * tdec, D_MODEL)
    mem2 = mem_prompt.reshape(nbp * N_MEM, D_MODEL)
    zeros_h = jnp.zeros((nbp, SSD_INNER, D_STATE), F32)
    zeros_c = jnp.zeros((nbp, 8, CONV_DIM), F32)

    outs = [[] for _ in range(10)]
    for l in range(DEPTH):
        wl = {
            'w_z': w_in_b[l, :, :sp[0]], 'w_xbc': w_in_b[l, :, sp[0]:sp[1]],
            'w_dt': _pad_cols(w_in_b[l, :, sp[1]:sp[2]], LANES),
            'w_q': w_in_b[l, :, sp[2]:sp[3]], 'w_k': w_in_b[l, :, sp[3]:sp[4]],
            'w_v': w_in_b[l, :, sp[4]:sp[5]], 'w_g': w_in_b[l, :, sp[5]:sp[6]],
            'w_ssd_br': w_ssd_br[l].astype(BF16), 'w_att_br': w_att_br[l].astype(BF16),
            'w_o': w_o[l].astype(BF16), 'w_cq': w_cq[l].astype(BF16), 'w_co': w_co[l].astype(BF16),
            'w_up': w_up[l].astype(BF16), 'w_down': w_down[l].astype(BF16),
            'ln1_g': ln1_g[l][None], 'ln1_b': ln1_b[l][None], 'ln2_g': ln2_g[l][None],
            'ln2_b': ln2_b[l][None], 'ln3_g': ln3_g[l][None], 'ln3_b': ln3_b[l][None],
        }
        ssd_p = (conv_w[l], conv_b[l][None], _pad_cols(dt_bias[l][None], LANES),
                 _pad_cols(a_log[l][None], LANES), jnp.repeat(d_skip[l], SSD_HEADDIM)[None],
                 ssd_norm_w[l][None])
        lam_init = 0.8 - 0.6 * math.exp(-0.3 * l)
        lq = lambda_q[l].astype(F32)
        lk = lambda_k[l].astype(F32)
        lam = jnp.exp(jnp.sum(lq[0] * lk[0])) - jnp.exp(jnp.sum(lq[1] * lk[1])) + lam_init
        lam2 = jnp.stack([lam, jnp.asarray(1.0 - lam_init, F32)]).astype(F32)
        sw = subln_w[l][None]

        mk_p = _mm(mem2, w_ck[l].astype(BF16), F32, 512, 1024)
        mv_p = _mm(mem2, w_cv[l].astype(BF16), F32, 512, 1024)
        z, xbc, dtr, g = _ssd_gate_inputs(xp, wl, 1024)
        qt, k, kb, v, vt = _qkv(xp, w_in_b[l, :, sp[2]:sp[5]], rope_p, nbp, 512, Q_SCALE)
        y, h_p = _ssd(xbc.reshape(nbp, seq, CONV_DIM), dtr.reshape(nbp, seq, LANES),
                      z.reshape(nbp, seq, SSD_INNER), zeros_h, zeros_c, *ssd_p, consts_p, valid=SSD_CHUNK)
        o_t = _flash(lam2, qt.reshape(nbp, DA_KV_HEADS, DA_GROUP, LANES, seq), kb.reshape(nbp, seq, kvw), vt,
                     jnp.broadcast_to(subln_w[l][:, None], (DA_DV, LANES)), FLASH_BQ, FLASH_BK)
        xp = _post_mixer(xp, y.reshape(nbp * seq, SSD_INNER), o_t.reshape(nbp, D_MODEL, seq), g,
                         mk_p.astype(BF16).reshape(nbp, N_MEM, D_MODEL),
                         mv_p.astype(BF16).reshape(nbp, N_MEM, D_MODEL), wl, 512, 512)
        c_p = xbc.reshape(nbp, seq, CONV_DIM)[:, seq - (CONV_W - 1):].astype(F32)

        zs, xbcs, dtrs, gs = _ssd_gate_inputs(xs, wl, LANES)
        qs, ks, vs = _sample_qkv(xs, wl, rope_s, LANES)
        pad_t = lambda a: jnp.pad(a.reshape(nbs, tdec, -1), ((0, 0), (0, SAMPLE_CHUNK - tdec), (0, 0)))
        cprev = jnp.pad(state_conv[l], ((0, 0), (8 - (CONV_W - 1), 0), (0, 0)))
        ys, h_s = _ssd(pad_t(xbcs), pad_t(dtrs), pad_t(zs),
                       state_ssm[l].reshape(nbs, SSD_INNER, D_STATE), cprev, *ssd_p, consts_s, valid=tdec)
        ys = ys[:, :tdec].reshape(nbs * tdec, SSD_INNER)
        c_s = jnp.concatenate([state_conv[l], xbcs.reshape(nbs, tdec, CONV_DIM).astype(F32)],
                              axis=1)[:, -(CONV_W - 1):]

        q6 = qs.astype(F32).reshape(nbs, tdec, DA_KV_HEADS, DA_GROUP, 2, DA_DH)
        q8 = jnp.transpose(q6, (0, 2, 4, 1, 3, 5)).reshape(nbs, DA_KV_HEADS * 2, tdec * DA_GROUP, DA_DH)
        qmat = jnp.einsum('bktd,kj->bktjd', q8, jnp.eye(DA_KV_HEADS * 2, dtype=F32))
        qmat = qmat.reshape(nbs, nrow, kvw).astype(BF16)
        knew_t = _pad_cols(jnp.transpose(ks.reshape(nbs, tdec, kvw), (0, 2, 1)), LANES)
        vnew = jnp.pad(vs.reshape(nbs, tdec, kvw), ((0, 0), (0, LANES - tdec), (0, 0)))
        o_s = _paged(page_table, jnp.full((1,), l, jnp.int32), lam2, qmat,
                     cache_kt, cache_v2, knew_t, vnew, nmask, sw)
        o_s = jnp.transpose(o_s.reshape(nbs, DA_KV_HEADS, tdec, DA_GROUP, DA_DV), (0, 2, 1, 3, 4))
        o_s = o_s.reshape(nbs * tdec, D_MODEL).astype(BF16)
        xs = _post_mixer(xs, ys, o_s, gs, cache_mem_k[l].astype(BF16).reshape(nbs, N_MEM, D_MODEL),
                         cache_mem_v[l].astype(BF16).reshape(nbs, N_MEM, D_MODEL), wl, LANES, 16)

        vals = (k.reshape(nbp, seq, DA_KV_HEADS, 2, DA_DH), v.reshape(nbp, seq, DA_KV_HEADS, DA_DV),
                h_p.reshape(nbp, SSD_HEADS, SSD_HEADDIM, D_STATE), c_p,
                mk_p.reshape(nbp, N_MEM, CA_HEADS, CA_DH), mv_p.reshape(nbp, N_MEM, CA_HEADS, CA_DH),
                ks.reshape(nbs, tdec, DA_KV_HEADS, 2, DA_DH), vs.reshape(nbs, tdec, DA_KV_HEADS, DA_DV),
                h_s.reshape(nbs, SSD_HEADS, SSD_HEADDIM, D_STATE), c_s)
        for lst, val in zip(outs, vals):
            lst.append(val)

    return (xp.reshape(nbp, seq, D_MODEL), xs.reshape(nbs, tdec, D_MODEL)) + tuple(jnp.stack(o) for o in outs)
```

```python
import functools
import math

import jax
import jax.numpy as jnp
import numpy as np
from jax import lax
from jax.experimental import pallas as pl
from jax.experimental.pallas import tpu as pltpu

F32 = jnp.float32
BF16 = jnp.bfloat16

D_MODEL = 1024
DEPTH = 4
PAST_LEN = 8192
PAGE_SIZE = 128
SSD_INNER = 2 * D_MODEL
SSD_HEADDIM = 64
SSD_HEADS = SSD_INNER // SSD_HEADDIM
SSD_GROUPS = 4
SSD_HPG = SSD_HEADS // SSD_GROUPS
D_STATE = 128
CONV_W = 4
CONV_DIM = SSD_INNER + 2 * SSD_GROUPS * D_STATE
SSD_CHUNK = 128
SAMPLE_CHUNK = 16
DA_HEADS = 8
DA_KV_HEADS = 4
DA_GROUP = DA_HEADS // DA_KV_HEADS
DA_DH = 64
DA_DV = 2 * DA_DH
ROPE_DIM = DA_DH // 4
ROPE_THETA = 500000.0
N_MEM = 256
CA_HEADS = 4
CA_DH = D_MODEL // CA_HEADS
D_FF = 4 * D_MODEL
IN_SPLITS = (SSD_INNER, CONV_DIM, SSD_HEADS, DA_HEADS * 2 * DA_DH,
             DA_KV_HEADS * 2 * DA_DH, DA_KV_HEADS * DA_DV, 2 * D_MODEL)
DN_ALPHA = (2 * DEPTH) ** 0.25
LN_EPS = 1e-5
RMS_EPS = 1e-6

LANES = 128
NEG_BIG = -1e30
VMEM_LIMIT = 56 * 1024 * 1024
PAGES_PER_STEP = 32
FLASH_BQ = 1024
FLASH_BK = 1024
ONES_ROWS = 16
Q_SCALE = math.log2(math.e) * DA_DH ** -0.5

NT_DIMS = (((1,), (1,)), ((), ()))
TN_DIMS = (((0,), (0,)), ((), ()))


def _params(sem):
    return pltpu.CompilerParams(dimension_semantics=sem, vmem_limit_bytes=VMEM_LIMIT)


def _dot(a, b):
    return jnp.dot(a, b, preferred_element_type=F32)


def _dot_nt(a, b):
    return lax.dot_general(a, b, NT_DIMS, preferred_element_type=F32)


def _dot_tn(a, b):
    return lax.dot_general(a, b, TN_DIMS, preferred_element_type=F32)


def _split3(a):
    hi = a.astype(BF16)
    r = a - hi.astype(F32)
    mid = r.astype(BF16)
    lo = (r - mid.astype(F32)).astype(BF16)
    return hi, mid, lo


def _sigmoid(x):
    return 1.0 / (1.0 + jnp.exp(-x))


def _layer_norm(v, g, b):
    mu = jnp.mean(v, axis=-1, keepdims=True)
    d = v - mu
    var = jnp.mean(d * d, axis=-1, keepdims=True)
    return d * lax.rsqrt(var + LN_EPS) * g + b


def _mm_kernel(x_ref, w_ref, o_ref):
    o_ref[...] = _dot(x_ref[...].astype(BF16), w_ref[...]).astype(o_ref.dtype)


def _mm_rope_kernel(x_ref, w_ref, c_ref, s1_ref, s2_ref, o_ref):
    acc = _dot(x_ref[...].astype(BF16), w_ref[...])
    c, s1, s2 = c_ref[...], s1_ref[...], s2_ref[...]
    for cb in range(acc.shape[1] // LANES):
        blk = acc[:, cb * LANES:(cb + 1) * LANES]
        up = pltpu.roll(blk, LANES - ROPE_DIM // 2, 1)
        dn = pltpu.roll(blk, ROPE_DIM // 2, 1)
        o_ref[:, cb * LANES:(cb + 1) * LANES] = (blk * c + up * s1 + dn * s2).astype(o_ref.dtype)


def _mm(x, w, out_dtype, bm, bn, rope=None):
    m, k = x.shape
    n = w.shape[1]
    bm, bn = min(bm, m), min(bn, n)
    if rope is not None:
        bm = min(bm, rope[0].shape[0])
    in_specs = [pl.BlockSpec((bm, k), lambda i, j: (i, 0)),
                pl.BlockSpec((k, bn), lambda i, j: (0, j))]
    args = [x, w]
    body = _mm_kernel
    if rope is not None:
        nrep = rope[0].shape[0] // bm
        for t in rope:
            in_specs.append(pl.BlockSpec((bm, LANES), lambda i, j: (i % nrep, 0)))
            args.append(t)
        body = _mm_rope_kernel
    return pl.pallas_call(
        body, grid=(m // bm, n // bn), in_specs=in_specs,
        out_specs=pl.BlockSpec((bm, bn), lambda i, j: (i, j)),
        out_shape=jax.ShapeDtypeStruct((m, n), out_dtype),
        compiler_params=_params(("parallel", "parallel")), name="proj",
    )(*args)


def _qkv_kernel(x_ref, w_ref, c_ref, s1_ref, s2_ref, qt_ref, k_ref, kb_ref, v_ref, vt_ref, *, q_scale):
    acc = _dot(x_ref[...].astype(BF16), w_ref[...])
    c, s1, s2 = c_ref[...], s1_ref[...], s2_ref[...]
    nq, nk = DA_HEADS * 2 * DA_DH // LANES, DA_KV_HEADS * 2 * DA_DH // LANES

    def rope(cb):
        blk = acc[:, cb * LANES:(cb + 1) * LANES]
        up = pltpu.roll(blk, LANES - ROPE_DIM // 2, 1)
        dn = pltpu.roll(blk, ROPE_DIM // 2, 1)
        return blk * c + up * s1 + dn * s2

    for cb in range(nq):
        qt_ref[0, cb * LANES:(cb + 1) * LANES, :] = (rope(cb) * q_scale).T.astype(BF16)
    for cb in range(nk):
        kblk = rope(nq + cb)
        k_ref[:, cb * LANES:(cb + 1) * LANES] = kblk
        kb_ref[:, cb * LANES:(cb + 1) * LANES] = kblk.astype(BF16)
    ones = jnp.ones((ONES_ROWS, acc.shape[0]), BF16)
    for h in range(DA_KV_HEADS):
        vblk = acc[:, (nq + nk + h) * LANES:(nq + nk + h + 1) * LANES]
        v_ref[:, h * DA_DV:(h + 1) * DA_DV] = vblk
        vt_ref[0, h, :DA_DV, :] = vblk.T.astype(BF16)
        vt_ref[0, h, DA_DV:, :] = ones


def _qkv(x, w_qkv, rope, nb, bm, q_scale):
    m, kdim = x.shape
    seq = m // nb
    bm = min(bm, seq)
    per_seq = seq // bm
    qw, kw = DA_HEADS * 2 * DA_DH, DA_KV_HEADS * 2 * DA_DH
    vw = DA_KV_HEADS * DA_DV
    row = lambda w: pl.BlockSpec((bm, w), lambda i: (i, 0))
    tab = pl.BlockSpec((bm, LANES), lambda i: (i % per_seq, 0))
    return pl.pallas_call(
        functools.partial(_qkv_kernel, q_scale=q_scale), grid=(m // bm,),
        in_specs=[row(kdim), pl.BlockSpec(w_qkv.shape, lambda i: (0, 0)), tab, tab, tab],
        out_specs=[pl.BlockSpec((1, qw, bm), lambda i: (i // per_seq, 0, i % per_seq)),
                   row(kw), row(kw), row(vw),
                   pl.BlockSpec((1, DA_KV_HEADS, DA_DV + ONES_ROWS, bm),
                                lambda i: (i // per_seq, 0, 0, i % per_seq))],
        out_shape=[jax.ShapeDtypeStruct((nb, qw, seq), BF16),
                   jax.ShapeDtypeStruct((m, kw), F32), jax.ShapeDtypeStruct((m, kw), BF16),
                   jax.ShapeDtypeStruct((m, vw), F32),
                   jax.ShapeDtypeStruct((nb, DA_KV_HEADS, DA_DV + ONES_ROWS, seq), BF16)],
        compiler_params=_params(("parallel",)), name="qkv_proj",
    )(x, w_qkv, *rope)


def _mm_res_ln_kernel(a_ref, w_ref, x_ref, g_ref, b_ref, o_ref):
    y = _dot(a_ref[...], w_ref[...])
    o_ref[...] = _layer_norm(DN_ALPHA * x_ref[...] + y, g_ref[...], b_ref[...])


def _mm_res_ln(a, w, x, g, b, bm):
    m, k = a.shape
    n = w.shape[1]
    bm = min(bm, m)
    return pl.pallas_call(
        _mm_res_ln_kernel, grid=(m // bm,),
        in_specs=[pl.BlockSpec((bm, k), lambda i: (i, 0)),
                  pl.BlockSpec((k, n), lambda i: (0, 0)),
                  pl.BlockSpec((bm, n), lambda i: (i, 0)),
                  pl.BlockSpec((1, n), lambda i: (0, 0)),
                  pl.BlockSpec((1, n), lambda i: (0, 0))],
        out_specs=pl.BlockSpec((bm, n), lambda i: (i, 0)),
        out_shape=jax.ShapeDtypeStruct((m, n), F32),
        compiler_params=_params(("parallel",)), name="proj_res_ln",
    )(a, w, x, g, b)


def _merge_kernel(y_ref, o_ref, g_ref, x_ref, wbr_ref, watt_ref, wo_ref, lg_ref, lb_ref, out_ref,
                  *, o_transposed):
    ssd = _dot(y_ref[...], wbr_ref[...])
    if o_transposed:
        att = _dot_tn(o_ref[0], watt_ref[...])
    else:
        att = _dot(o_ref[...], watt_ref[...])
    g = g_ref[...].astype(F32)
    merged = _sigmoid(g[:, :D_MODEL]) * ssd + _sigmoid(g[:, D_MODEL:]) * att
    mix = _dot(merged.astype(BF16), wo_ref[...])
    out_ref[...] = _layer_norm(DN_ALPHA * x_ref[...] + mix, lg_ref[...], lb_ref[...])


def _merge(y, o, g, x, wbr, watt, wo, lg, lb, bm):
    m = x.shape[0]
    bm = min(bm, m)
    row = lambda w: pl.BlockSpec((bm, w), lambda i: (i, 0))
    full = lambda a: pl.BlockSpec(a.shape, lambda i: (0, 0))
    o_transposed = o.ndim == 3
    if o_transposed:
        per_seq = o.shape[2] // bm
        o_spec = pl.BlockSpec((1, D_MODEL, bm), lambda i: (i // per_seq, 0, i % per_seq))
    else:
        o_spec = row(D_MODEL)
    return pl.pallas_call(
        functools.partial(_merge_kernel, o_transposed=o_transposed), grid=(m // bm,),
        in_specs=[row(SSD_INNER), o_spec, row(2 * D_MODEL), row(D_MODEL),
                  full(wbr), full(watt), full(wo), full(lg), full(lb)],
        out_specs=row(D_MODEL),
        out_shape=jax.ShapeDtypeStruct((m, D_MODEL), F32),
        compiler_params=_params(("parallel",)), name="merge",
    )(y, o, g, x, wbr, watt, wo, lg, lb)


def _mlp_kernel(x_ref, wu_ref, wd_ref, lg_ref, lb_ref, o_ref, acc_sc, xb_sc):
    j = pl.program_id(1)

    @pl.when(j == 0)
    def _():
        acc_sc[...] = jnp.zeros_like(acc_sc)
        xb_sc[...] = x_ref[...].astype(BF16)

    h = jnp.maximum(_dot(xb_sc[...], wu_ref[...]), 0.0)
    acc_sc[...] += _dot((h * h).astype(BF16), wd_ref[...])

    @pl.when(j == pl.num_programs(1) - 1)
    def _():
        o_ref[...] = _layer_norm(DN_ALPHA * x_ref[...] + acc_sc[...], lg_ref[...], lb_ref[...])


def _mlp(x, wu, wd, lg, lb, bm, bf):
    m = x.shape[0]
    bm = min(bm, m)
    return pl.pallas_call(
        _mlp_kernel, grid=(m // bm, D_FF // bf),
        in_specs=[pl.BlockSpec((bm, D_MODEL), lambda i, j: (i, 0)),
                  pl.BlockSpec((D_MODEL, bf), lambda i, j: (0, j)),
                  pl.BlockSpec((bf, D_MODEL), lambda i, j: (j, 0)),
                  pl.BlockSpec((1, D_MODEL), lambda i, j: (0, 0)),
                  pl.BlockSpec((1, D_MODEL), lambda i, j: (0, 0))],
        out_specs=pl.BlockSpec((bm, D_MODEL), lambda i, j: (i, 0)),
        out_shape=jax.ShapeDtypeStruct((m, D_MODEL), F32),
        scratch_shapes=[pltpu.VMEM((bm, D_MODEL), F32), pltpu.VMEM((bm, D_MODEL), BF16)],
        compiler_params=_params(("parallel", "arbitrary")), name="mlp",
    )(x, wu, wd, lg, lb)


def _cross_kernel(q_ref, k_ref, v_ref, o_ref):
    q = q_ref[0]
    k = k_ref[0]
    v = v_ref[0]
    for h in range(CA_HEADS):
        sl = slice(h * CA_DH, (h + 1) * CA_DH)
        s = _dot_nt(q[:, sl], k[:, sl]) * (CA_DH ** -0.5)
        s = s - jnp.max(s, axis=-1, keepdims=True)
        p = jnp.exp(s)
        p = p / jnp.sum(p, axis=-1, keepdims=True)
        o_ref[0, :, sl] = _dot(p.astype(BF16), v[:, sl]).astype(o_ref.dtype)


def _cross(q, mk, mv, tq):
    b, t, d = q.shape
    tq = min(tq, t)
    return pl.pallas_call(
        _cross_kernel, grid=(b, t // tq),
        in_specs=[pl.BlockSpec((1, tq, d), lambda i, j: (i, j, 0)),
                  pl.BlockSpec((1, N_MEM, d), lambda i, j: (i, 0, 0)),
                  pl.BlockSpec((1, N_MEM, d), lambda i, j: (i, 0, 0))],
        out_specs=pl.BlockSpec((1, tq, d), lambda i, j: (i, j, 0)),
        out_shape=jax.ShapeDtypeStruct((b, t, d), BF16),
        compiler_params=_params(("parallel", "parallel")), name="cross_attn",
    )(q, mk, mv)


def _ssd_kernel(xbc_ref, dt_ref, z_ref, h0_ref, cprev_ref, cw_ref, cb_ref, dtb_ref, alog_ref,
                dskip_ref, nw_ref, tri_ref, eye_ref, exp_ref, y_ref, hout_ref,
                h_sc, xpad_sc, y_sc, *, q_len, valid):
    c = pl.program_id(1)

    @pl.when(c == 0)
    def _():
        h_sc[...] = h0_ref[0]
        xpad_sc[0:8, :] = cprev_ref[0]

    xpad_sc[8:8 + q_len, :] = xbc_ref[0].astype(F32)
    cw = cw_ref[...]
    conv = cb_ref[...] + cw[3:4, :] * xpad_sc[8:8 + q_len, :]
    for kk in range(1, CONV_W):
        conv = conv + cw[3 - kk:4 - kk, :] * xpad_sc[8 - kk:8 - kk + q_len, :]
    xpad_sc[0:8, :] = xpad_sc[q_len:q_len + 8, :]
    xbc = conv * _sigmoid(conv)
    xs = xbc[:, :SSD_INNER]

    dtr = dt_ref[0] + dtb_ref[...]
    dtv = jnp.maximum(dtr, 0.0) + jnp.log(1.0 + jnp.exp(-jnp.abs(dtr)))
    if valid < q_len:
        rows = lax.broadcasted_iota(jnp.int32, dtv.shape, 0)
        dtv = jnp.where(rows < valid, dtv, 0.0)
    a = dtv * (-jnp.exp(alog_ref[...]) * math.log2(math.e))

    tri = tri_ref[...]
    eye = eye_ref[...]
    expand = exp_ref[...]
    a_parts = _split3(a)
    acum = sum(_dot(tri, p) for p in a_parts)
    a_t = sum(_dot_nt(eye, p) for p in a_parts)
    acum_t = sum(_dot_nt(p, tri) for p in _split3(a_t))
    acum_x = sum(_dot(p, expand) for p in _split3(acum))
    dt_x = sum(_dot(p, expand) for p in _split3(dtv))

    xdt = xs * dt_x
    xdt_b = xdt.astype(BF16)
    xdec = (xdt * jnp.exp2(acum_x[q_len - 1:q_len, :] - acum_x)).astype(BF16)
    eacum_x = jnp.exp2(acum_x)
    chunk_decay = jnp.exp2(jnp.broadcast_to(acum_t[:, q_len - 1:q_len], (LANES, LANES)))

    li = lax.broadcasted_iota(jnp.int32, (q_len, q_len), 0)
    si = lax.broadcasted_iota(jnp.int32, (q_len, q_len), 1)
    causal = li >= si
    left = lax.broadcasted_iota(jnp.int32, (q_len, LANES), 1) < SSD_HEADDIM
    gw = SSD_HPG * SSD_HEADDIM
    for g in range(SSD_GROUPS):
        b_g = xbc[:, SSD_INNER + g * D_STATE:SSD_INNER + (g + 1) * D_STATE].astype(BF16)
        c_off = SSD_INNER + SSD_GROUPS * D_STATE
        c_g = xbc[:, c_off + g * D_STATE:c_off + (g + 1) * D_STATE].astype(BF16)
        cbm = _dot_nt(c_g, b_g)
        h_g = h_sc[g * gw:(g + 1) * gw, :]
        y_off = _dot_nt(c_g, h_g.astype(BF16))
        for pp in range(SSD_HPG // 2):
            j0 = g * SSD_HPG + 2 * pp
            cols = slice(j0 * SSD_HEADDIM, (j0 + 2) * SSD_HEADDIM)
            x_pair = xdt_b[:, cols]
            halves = []
            for j in (j0, j0 + 1):
                seg = acum[:, j:j + 1] - acum_t[j:j + 1, :]
                m_j = (cbm * jnp.where(causal, jnp.exp2(seg), 0.0)).astype(BF16)
                halves.append(_dot(m_j, x_pair))
            y_diag = jnp.where(left, halves[0], halves[1])
            y_sc[:, cols] = (y_diag + y_off[:, pp * LANES:(pp + 1) * LANES] * eacum_x[:, cols]
                             + xs[:, cols] * dskip_ref[:, cols])
        new_states = _dot_tn(xdec[:, g * gw:(g + 1) * gw], b_g)
        for jj in range(SSD_HPG):
            j = g * SSD_HPG + jj
            r = slice(j * SSD_HEADDIM, (j + 1) * SSD_HEADDIM)
            rl = slice(jj * SSD_HEADDIM, (jj + 1) * SSD_HEADDIM)
            h_sc[r, :] = h_g[rl, :] * chunk_decay[j:j + 1, :] + new_states[rl, :]

    z = z_ref[0].astype(F32)
    yz = y_sc[...] * (z * _sigmoid(z))
    ms = jnp.mean(yz * yz, axis=-1, keepdims=True)
    y_ref[0] = (yz * lax.rsqrt(ms + RMS_EPS) * nw_ref[...]).astype(y_ref.dtype)

    @pl.when(c == pl.num_programs(1) - 1)
    def _():
        hout_ref[0] = h_sc[...]


def _ssd(xbc, dtr, z, h0, cprev, cw, cb, dtb, alog, dskip_x, nw, consts, valid):
    b, l, _ = xbc.shape
    tri, eye, expand = consts
    q_len = tri.shape[0]
    nc = l // q_len
    tok = lambda w: pl.BlockSpec((1, q_len, w), lambda i, c: (i, c, 0))
    per_b = lambda r, w: pl.BlockSpec((1, r, w), lambda i, c: (i, 0, 0))
    full = lambda a: pl.BlockSpec(a.shape, lambda i, c: (0,) * a.ndim)
    return pl.pallas_call(
        functools.partial(_ssd_kernel, q_len=q_len, valid=valid),
        grid=(b, nc),
        in_specs=[tok(CONV_DIM), tok(LANES), tok(SSD_INNER),
                  per_b(SSD_INNER, D_STATE), per_b(8, CONV_DIM),
                  full(cw), full(cb), full(dtb), full(alog), full(dskip_x), full(nw),
                  full(tri), full(eye), full(expand)],
        out_specs=[tok(SSD_INNER), per_b(SSD_INNER, D_STATE)],
        out_shape=[jax.ShapeDtypeStruct((b, l, SSD_INNER), BF16),
                   jax.ShapeDtypeStruct((b, SSD_INNER, D_STATE), F32)],
        scratch_shapes=[pltpu.VMEM((SSD_INNER, D_STATE), F32),
                        pltpu.VMEM((q_len + 8, CONV_DIM), F32),
                        pltpu.VMEM((q_len, SSD_INNER), F32)],
        compiler_params=_params(("parallel", "arbitrary")), name="ssd",
    )(xbc, dtr, z, h0, cprev, cw, cb, dtb, alog, dskip_x, nw, tri, eye, expand)


def _flash_kernel(qm_ref, km_ref, lam_ref, qt_ref, k_ref, vt_ref, sw_ref, o_ref,
                  qw_sc, m_sc, acc_sc, *, bq, bk):
    step = pl.program_id(2)
    qi = qm_ref[step]
    kj = km_ref[step]
    cols = DA_GROUP * bq

    @pl.when(kj == 0)
    def _():
        row = lax.broadcasted_iota(jnp.int32, (LANES, bq), 0)
        for g in range(DA_GROUP):
            qg = qt_ref[0, 0, g].astype(F32)
            qw_sc[0, :, g * bq:(g + 1) * bq] = jnp.where(row < DA_DH, qg, 0.0).astype(BF16)
            qw_sc[1, :, g * bq:(g + 1) * bq] = jnp.where(row >= DA_DH, qg, 0.0).astype(BF16)
        m_sc[...] = jnp.full(m_sc.shape, NEG_BIG, F32)
        acc_sc[...] = jnp.zeros_like(acc_sc)

    def update(krows, cslices, masked):
        k = k_ref[0, krows, :]
        vt = vt_ref[0, 0, :, krows]
        nk = krows.stop - krows.start
        jobs = [(i, cs) for i in range(2) for cs in cslices]
        scores = [_dot(k, qw_sc[i, :, cs]) for i, cs in jobs]
        for (i, cs), st in zip(jobs, scores):
            if masked:
                w = cs.stop - cs.start
                key_minus_query = (kj * bk + krows.start - qi * bq - cs.start % bq
                                   + lax.broadcasted_iota(jnp.int32, (nk, w), 0)
                                   - lax.broadcasted_iota(jnp.int32, (nk, w), 1))
                st = jnp.where(key_minus_query <= 0, st, NEG_BIG)
            m_prev = m_sc[i, :, cs]
            m_new = jnp.maximum(m_prev, jnp.max(st, axis=0, keepdims=True))
            p = jnp.exp2(st - m_new).astype(BF16)
            acc_sc[i, :, cs] = jnp.exp2(m_prev - m_new) * acc_sc[i, :, cs] + _dot(vt, p)
            m_sc[i, :, cs] = m_new

    groups = [slice(g * bq, (g + 1) * bq) for g in range(DA_GROUP)]
    on_diag = kj * bk + bk - 1 > qi * bq

    @pl.when(on_diag)
    def _():
        if bq == bk:
            update(slice(0, bk // 2), groups, True)
            update(slice(bk // 2, bk), [slice(g.start + bq // 2, g.stop) for g in groups], True)
        else:
            update(slice(0, bk), groups, True)

    pl.when(jnp.logical_not(on_diag))(functools.partial(update, slice(0, bk), [slice(0, cols)], False))

    @pl.when(kj == (qi + 1) * (bq // bk) - 1)
    def _():
        lam = lam_ref[0]
        post = lam_ref[1]
        sw = jnp.concatenate([sw_ref[...]] * (bq // LANES), axis=1)
        for g in range(DA_GROUP):
            cs = slice(g * bq, (g + 1) * bq)
            o = (acc_sc[0, :DA_DV, cs] / acc_sc[0, DA_DV:DA_DV + 1, cs]
                 - lam * (acc_sc[1, :DA_DV, cs] / acc_sc[1, DA_DV:DA_DV + 1, cs]))
            ms = jnp.mean(o * o, axis=0, keepdims=True)
            o_ref[0, 0, g] = (o * lax.rsqrt(ms + RMS_EPS) * sw * post).astype(o_ref.dtype)


def _flash(lam2, qt, k, vt, sw_b, bq, bk):
    b, s, _ = k.shape
    dva = vt.shape[2]
    bq, bk = min(bq, s), min(bk, s)
    assert bq % bk == 0 and s % bq == 0 and bq % LANES == 0
    ratio = bq // bk
    pairs = [(qi, kj) for qi in range(s // bq) for kj in range((qi + 1) * ratio)]
    qmap = jnp.asarray([p[0] for p in pairs], jnp.int32)
    kmap = jnp.asarray([p[1] for p in pairs], jnp.int32)
    cols = DA_GROUP * bq
    grid_spec = pltpu.PrefetchScalarGridSpec(
        num_scalar_prefetch=2, grid=(b, DA_KV_HEADS, len(pairs)),
        in_specs=[pl.BlockSpec(memory_space=pltpu.SMEM),
                  pl.BlockSpec((1, 1, DA_GROUP, LANES, bq), lambda bi, h, st, qm, km: (bi, h, 0, 0, qm[st])),
                  pl.BlockSpec((1, bk, LANES), lambda bi, h, st, qm, km: (bi, km[st], h)),
                  pl.BlockSpec((1, 1, dva, bk), lambda bi, h, st, qm, km: (bi, h, 0, km[st])),
                  pl.BlockSpec((DA_DV, LANES), lambda bi, h, st, qm, km: (0, 0))],
        out_specs=pl.BlockSpec((1, 1, DA_GROUP, DA_DV, bq), lambda bi, h, st, qm, km: (bi, h, 0, 0, qm[st])),
        scratch_shapes=[pltpu.VMEM((2, LANES, cols), BF16),
                        pltpu.VMEM((2, 1, cols), F32),
                        pltpu.VMEM((2, dva, cols), F32)])
    return pl.pallas_call(
        functools.partial(_flash_kernel, bq=bq, bk=bk), grid_spec=grid_spec,
        out_shape=jax.ShapeDtypeStruct((b, DA_KV_HEADS, DA_GROUP, DA_DV, s), BF16),
        compiler_params=_params(("parallel", "parallel", "arbitrary")), name="flash_diff_attn",
    )(qmap, kmap, lam2, qt, k, vt, sw_b)


def _paged_kernel(pt_ref, lay_ref, lam_ref, q_ref, *rest, n_pages, npg):
    kp, vp = rest[:npg], rest[npg:2 * npg]
    knew_ref, vnew_ref, nmask_ref, sw_ref, o_ref, m_sc, l_sc, acc_sc = rest[2 * npg:]
    step = pl.program_id(1)
    n_steps = n_pages // npg
    hrows = q_ref.shape[1] // DA_KV_HEADS

    @pl.when(step == 0)
    def _():
        m_sc[...] = jnp.full(m_sc.shape, NEG_BIG, F32)
        l_sc[...] = jnp.zeros_like(l_sc)
        acc_sc[...] = jnp.zeros_like(acc_sc)

    def update(s, v_of_head):
        m_prev = m_sc[...]
        m_new = jnp.maximum(m_prev, jnp.max(s, axis=-1, keepdims=True))
        alpha = jnp.exp2(m_prev - m_new)
        p = jnp.exp2(s - jnp.concatenate([m_new] * (s.shape[1] // LANES), axis=1))
        l_sc[...] = alpha * l_sc[...] + jnp.sum(p, axis=-1, keepdims=True)
        pb = p.astype(BF16)
        for h in range(DA_KV_HEADS):
            rs = slice(h * hrows, (h + 1) * hrows)
            acc_sc[rs, :] = alpha[rs, :] * acc_sc[rs, :] + _dot(pb[rs, :], v_of_head(h))
        m_sc[...] = m_new

    @pl.when(step < n_steps)
    def _():
        kt = jnp.concatenate([r[0, 0].astype(BF16) for r in kp], axis=1)
        s = _dot(q_ref[0], kt)

        def v_of_head(h):
            return jnp.concatenate(
                [r[0, 0, pl.ds(h, PAGE_SIZE, stride=DA_KV_HEADS), :] for r in vp], axis=0).astype(BF16)

        update(s, v_of_head)

    @pl.when(step == n_steps)
    def _():
        s = _dot(q_ref[0], knew_ref[0].astype(BF16)) + nmask_ref[...]
        update(s, lambda h: vnew_ref[0][:, h * DA_DV:(h + 1) * DA_DV].astype(BF16))
        lam = lam_ref[0]
        post = lam_ref[1]
        half = hrows // 2
        for h in range(DA_KV_HEADS):
            r0 = slice(h * hrows, h * hrows + half)
            r1 = slice(h * hrows + half, (h + 1) * hrows)
            o = acc_sc[r0, :] / l_sc[r0, :] - lam * (acc_sc[r1, :] / l_sc[r1, :])
            ms = jnp.mean(o * o, axis=-1, keepdims=True)
            o_ref[0, h * half:(h + 1) * half, :] = o * lax.rsqrt(ms + RMS_EPS) * sw_ref[...] * post


def _paged(page_table, layer, lam2, qmat, cache_kt, cache_v2, knew_t, vnew, nmask, sw):
    nb, n_pages = page_table.shape
    npg = min(PAGES_PER_STEP, n_pages)
    assert n_pages % npg == 0
    n_steps = n_pages // npg
    nrows = qmat.shape[1]
    kvw = cache_kt.shape[2]

    def page_spec(o):
        def imap(b, s, pt, lay):
            return (lay[0], pt[b, jnp.minimum(s, n_steps - 1) * npg + o], 0, 0)
        return pl.BlockSpec((1, 1, kvw, LANES), imap)

    per_b = lambda a: pl.BlockSpec((1,) + a.shape[1:], lambda b, s, pt, lay: (b, 0, 0))
    full = lambda a: pl.BlockSpec(a.shape, lambda b, s, pt, lay: (0, 0))
    in_specs = ([pl.BlockSpec(memory_space=pltpu.SMEM), per_b(qmat)]
                + [page_spec(o) for o in range(npg)] + [page_spec(o) for o in range(npg)]
                + [per_b(knew_t), per_b(vnew), full(nmask), full(sw)])
    grid_spec = pltpu.PrefetchScalarGridSpec(
        num_scalar_prefetch=2, grid=(nb, n_steps + 1), in_specs=in_specs,
        out_specs=pl.BlockSpec((1, nrows // 2, DA_DV), lambda b, s, pt, lay: (b, 0, 0)),
        scratch_shapes=[pltpu.VMEM((nrows, LANES), F32), pltpu.VMEM((nrows, LANES), F32),
                        pltpu.VMEM((nrows, DA_DV), F32)])
    return pl.pallas_call(
        functools.partial(_paged_kernel, n_pages=n_pages, npg=npg), grid_spec=grid_spec,
        out_shape=jax.ShapeDtypeStruct((nb, nrows // 2, DA_DV), F32),
        compiler_params=_params(("parallel", "arbitrary")), name="paged_diff_attn",
    )(page_table, layer, lam2, qmat, *([cache_kt] * npg), *([cache_v2] * npg), knew_t, vnew, nmask, sw)


def _rope_tables(pos):
    half = ROPE_DIM // 2
    inv_freq = ROPE_THETA ** (-jnp.arange(half, dtype=F32) / half)
    ang = pos.astype(F32)[:, None] * inv_freq[None, :]
    cos, sin = jnp.cos(ang), jnp.sin(ang)
    t = pos.shape[0]
    pad = jnp.zeros((t, DA_DH - ROPE_DIM), F32)
    c = jnp.concatenate([cos, cos, jnp.ones_like(pad)], axis=1)
    s1 = jnp.concatenate([-sin, jnp.zeros_like(sin), pad], axis=1)
    s2 = jnp.concatenate([jnp.zeros_like(sin), sin, pad], axis=1)
    rep = LANES // DA_DH
    return tuple(jnp.tile(a, (1, rep)) for a in (c, s1, s2))


def _ssd_consts(q):
    tri = jnp.asarray(np.tril(np.ones((q, q), np.float32)), BF16)
    eye = jnp.asarray(np.eye(LANES, dtype=np.float32), BF16)
    expand = np.zeros((LANES, SSD_INNER), np.float32)
    for j in range(SSD_HEADS):
        expand[j, j * SSD_HEADDIM:(j + 1) * SSD_HEADDIM] = 1.0
    return tri, eye, jnp.asarray(expand, BF16)


def _pad_cols(a, width):
    return jnp.pad(a, [(0, 0)] * (a.ndim - 1) + [(0, width - a.shape[-1])])


def _ssd_gate_inputs(x2, wl, bm):
    z = _mm(x2, wl['w_z'], BF16, bm, 2048)
    xbc = _mm(x2, wl['w_xbc'], BF16, bm, 1536)
    dtr = _mm(x2, wl['w_dt'], F32, bm, LANES)
    g = _mm(x2, wl['w_g'], BF16, bm, 2048)
    return z, xbc, dtr, g


def _sample_qkv(x2, wl, rope, bm):
    q = _mm(x2, wl['w_q'], BF16, bm, 1024, tuple(t * Q_SCALE for t in rope))
    k = _mm(x2, wl['w_k'], F32, bm, 512, rope)
    v = _mm(x2, wl['w_v'], F32, bm, 512)
    return q, k, v


def _post_mixer(x2, y, o, g, mk, mv, wl, bm, tq):
    nb = mk.shape[0]
    x1 = _merge(y, o, g, x2, wl['w_ssd_br'], wl['w_att_br'], wl['w_o'], wl['ln1_g'], wl['ln1_b'], bm)
    cq = _mm(x1, wl['w_cq'], BF16, bm, 1024)
    t = cq.shape[0] // nb
    if t < 16:
        cq3 = jnp.pad(cq.reshape(nb, t, D_MODEL), ((0, 0), (0, 16 - t), (0, 0)))
        co = _cross(cq3, mk, mv, 16)[:, :t].reshape(nb * t, D_MODEL)
    else:
        co = _cross(cq.reshape(nb, t, D_MODEL), mk, mv, tq).reshape(nb * t, D_MODEL)
    x2n = _mm_res_ln(co, wl['w_co'], x1, wl['ln2_g'], wl['ln2_b'], bm)
    return _mlp(x2n, wl['w_up'], wl['w_down'], wl['ln3_g'], wl['ln3_b'], bm, 1024)


def kernel(x_prompt, x_sample, mem_prompt, cache_k, cache_v, state_ssm, state_conv, cache_mem_k, cache_mem_v, page_table, w_in, conv_w, conv_b, dt_bias, a_log, d_skip, ssd_norm_w, w_ssd_br, lambda_q, lambda_k, subln_w, w_att_br, w_o, ln1_g, ln1_b, w_cq, w_ck, w_cv, w_co, ln2_g, ln2_b, w_up, w_down, ln3_g, ln3_b):
    nbp, seq, _ = x_prompt.shape
    nbs, tdec, _ = x_sample.shape
    n_pool = cache_k.shape[1]
    kvw = DA_KV_HEADS * DA_DV

    sp = tuple(int(v) for v in np.cumsum(IN_SPLITS))
    w_in_b = w_in.astype(BF16)
    rope_p = _rope_tables(jnp.arange(seq, dtype=jnp.int32))
    rope_s = _rope_tables(jnp.tile(PAST_LEN + jnp.arange(tdec, dtype=jnp.int32), nbs))
    consts_p = _ssd_consts(SSD_CHUNK)
    consts_s = _ssd_consts(SAMPLE_CHUNK)

    cache_kt = jnp.transpose(cache_k, (0, 1, 3, 4, 5, 2)).reshape(DEPTH, n_pool, kvw, PAGE_SIZE)
    cache_v2 = cache_v.reshape(DEPTH, n_pool, PAGE_SIZE * DA_KV_HEADS, DA_DV)

    nrow = DA_KV_HEADS * 2 * tdec * DA_GROUP
    row_t = (np.arange(nrow) % (tdec * DA_GROUP)) // DA_GROUP
    nmask = jnp.asarray(np.where(np.arange(LANES)[None, :] <= row_t[:, None], 0.0, NEG_BIG), F32)

    xp = x_prompt.reshape(nbp * seq, D_MODEL)
    xs = x_sample.reshape(nbs * tdec, D_MODEL)
    mem2 = mem_prompt.reshape(nbp * N_MEM, D_MODEL)
    zeros_h = jnp.zeros((nbp, SSD_INNER, D_STATE), F32)
    zeros_c = jnp.zeros((nbp, 8, CONV_DIM), F32)

    outs = [[] for _ in range(10)]
    for l in range(DEPTH):
        wl = {
            'w_z': w_in_b[l, :, :sp[0]], 'w_xbc': w_in_b[l, :, sp[0]:sp[1]],
            'w_dt': _pad_cols(w_in_b[l, :, sp[1]:sp[2]], LANES),
            'w_q': w_in_b[l, :, sp[2]:sp[3]], 'w_k': w_in_b[l, :, sp[3]:sp[4]],
            'w_v': w_in_b[l, :, sp[4]:sp[5]], 'w_g': w_in_b[l, :, sp[5]:sp[6]],
            'w_ssd_br': w_ssd_br[l].astype(BF16), 'w_att_br': w_att_br[l].astype(BF16),
            'w_o': w_o[l].astype(BF16), 'w_cq': w_cq[l].astype(BF16), 'w_co': w_co[l].astype(BF16),
            'w_up': w_up[l].astype(BF16), 'w_down': w_down[l].astype(BF16),
            'ln1_g': ln1_g[l][None], 'ln1_b': ln1_b[l][None], 'ln2_g': ln2_g[l][None],
            'ln2_b': ln2_b[l][None], 'ln3_g': ln3_g[l][None], 'ln3_b': ln3_b[l][None],
        }
        ssd_p = (conv_w[l], conv_b[l][None], _pad_cols(dt_bias[l][None], LANES),
                 _pad_cols(a_log[l][None], LANES), jnp.repeat(d_skip[l], SSD_HEADDIM)[None],
                 ssd_norm_w[l][None])
        lam_init = 0.8 - 0.6 * math.exp(-0.3 * l)
        lq = lambda_q[l].astype(F32)
        lk = lambda_k[l].astype(F32)
        lam = jnp.exp(jnp.sum(lq[0] * lk[0])) - jnp.exp(jnp.sum(lq[1] * lk[1])) + lam_init
        lam2 = jnp.stack([lam, jnp.asarray(1.0 - lam_init, F32)]).astype(F32)
        sw = subln_w[l][None]

        mk_p = _mm(mem2, w_ck[l].astype(BF16), F32, 512, 1024)
        mv_p = _mm(mem2, w_cv[l].astype(BF16), F32, 512, 1024)
        z, xbc, dtr, g = _ssd_gate_inputs(xp, wl, 1024)
        qt, k, kb, v, vt = _qkv(xp, w_in_b[l, :, sp[2]:sp[5]], rope_p, nbp, 512, Q_SCALE)
        y, h_p = _ssd(xbc.reshape(nbp, seq, CONV_DIM), dtr.reshape(nbp, seq, LANES),
                      z.reshape(nbp, seq, SSD_INNER), zeros_h, zeros_c, *ssd_p, consts_p, valid=SSD_CHUNK)
        o_t = _flash(lam2, qt.reshape(nbp, DA_KV_HEADS, DA_GROUP, LANES, seq), kb.reshape(nbp, seq, kvw), vt,
                     jnp.broadcast_to(subln_w[l][:, None], (DA_DV, LANES)), FLASH_BQ, FLASH_BK)
        xp = _post_mixer(xp, y.reshape(nbp * seq, SSD_INNER), o_t.reshape(nbp, D_MODEL, seq), g,
                         mk_p.astype(BF16).reshape(nbp, N_MEM, D_MODEL),
                         mv_p.astype(BF16).reshape(nbp, N_MEM, D_MODEL), wl, 512, 512)
        c_p = xbc.reshape(nbp, seq, CONV_DIM)[:, seq - (CONV_W - 1):].astype(F32)

        zs, xbcs, dtrs, gs = _ssd_gate_inputs(xs, wl, LANES)
        qs, ks, vs = _sample_qkv(xs, wl, rope_s, LANES)
        pad_t = lambda a: jnp.pad(a.reshape(nbs, tdec, -1), ((0, 0), (0, SAMPLE_CHUNK - tdec), (0, 0)))
        cprev = jnp.pad(state_conv[l], ((0, 0), (8 - (CONV_W - 1), 0), (0, 0)))
        ys, h_s = _ssd(pad_t(xbcs), pad_t(dtrs), pad_t(zs),
                       state_ssm[l].reshape(nbs, SSD_INNER, D_STATE), cprev, *ssd_p, consts_s, valid=tdec)
        ys = ys[:, :tdec].reshape(nbs * tdec, SSD_INNER)
        c_s = jnp.concatenate([state_conv[l], xbcs.reshape(nbs, tdec, CONV_DIM).astype(F32)],
                              axis=1)[:, -(CONV_W - 1):]

        q6 = qs.astype(F32).reshape(nbs, tdec, DA_KV_HEADS, DA_GROUP, 2, DA_DH)
        q8 = jnp.transpose(q6, (0, 2, 4, 1, 3, 5)).reshape(nbs, DA_KV_HEADS * 2, tdec * DA_GROUP, DA_DH)
        qmat = jnp.einsum('bktd,kj->bktjd', q8, jnp.eye(DA_KV_HEADS * 2, dtype=F32))
        qmat = qmat.reshape(nbs, nrow, kvw).astype(BF16)
        knew_t = _pad_cols(jnp.transpose(ks.reshape(nbs, tdec, kvw), (0, 2, 1)), LANES)
        vnew = jnp.pad(vs.reshape(nbs, tdec, kvw), ((0, 0), (0, LANES - tdec), (0, 0)))
        o_s = _paged(page_table, jnp.full((1,), l, jnp.int32), lam2, qmat,
                     cache_kt, cache_v2, knew_t, vnew, nmask, sw)
        o_s = jnp.transpose(o_s.reshape(nbs, DA_KV_HEADS, tdec, DA_GROUP, DA_DV), (0, 2, 1, 3, 4))
        o_s = o_s.reshape(nbs * tdec, D_MODEL).astype(BF16)
        xs = _post_mixer(xs, ys, o_s, gs, cache_mem_k[l].astype(BF16).reshape(nbs, N_MEM, D_MODEL),
                         cache_mem_v[l].astype(BF16).reshape(nbs, N_MEM, D_MODEL), wl, LANES, 16)

        vals = (k.reshape(nbp, seq, DA_KV_HEADS, 2, DA_DH), v.reshape(nbp, seq, DA_KV_HEADS, DA_DV),
                h_p.reshape(nbp, SSD_HEADS, SSD_HEADDIM, D_STATE), c_p,
                mk_p.reshape(nbp, N_MEM, CA_HEADS, CA_DH), mv_p.reshape(nbp, N_MEM, CA_HEADS, CA_DH),
                ks.reshape(nbs, tdec, DA_KV_HEADS, 2, DA_DH), vs.reshape(nbs, tdec, DA_KV_HEADS, DA_DV),
                h_s.reshape(nbs, SSD_HEADS, SSD_HEADDIM, D_STATE), c_s)
        for lst, val in zip(outs, vals):
            lst.append(val)

    return (xp.reshape(nbp, seq, D_MODEL), xs.reshape(nbs, tdec, D_MODEL)) + tuple(jnp.stack(o) for o in outs)
```

```python
import functools
import math

import jax
import jax.numpy as jnp
import numpy as np
from jax import lax
from jax.experimental import pallas as pl
from jax.experimental.pallas import tpu as pltpu

F32 = jnp.float32
BF16 = jnp.bfloat16

D_MODEL = 1024
DEPTH = 4
PAST_LEN = 8192
PAGE_SIZE = 128
SSD_INNER = 2 * D_MODEL
SSD_HEADDIM = 64
SSD_HEADS = SSD_INNER // SSD_HEADDIM
SSD_GROUPS = 4
SSD_HPG = SSD_HEADS // SSD_GROUPS
D_STATE = 128
CONV_W = 4
CONV_DIM = SSD_INNER + 2 * SSD_GROUPS * D_STATE
SSD_CHUNK = 128
SAMPLE_CHUNK = 16
DA_HEADS = 8
DA_KV_HEADS = 4
DA_GROUP = DA_HEADS // DA_KV_HEADS
DA_DH = 64
DA_DV = 2 * DA_DH
ROPE_DIM = DA_DH // 4
ROPE_THETA = 500000.0
N_MEM = 256
CA_HEADS = 4
CA_DH = D_MODEL // CA_HEADS
D_FF = 4 * D_MODEL
IN_SPLITS = (SSD_INNER, CONV_DIM, SSD_HEADS, DA_HEADS * 2 * DA_DH,
             DA_KV_HEADS * 2 * DA_DH, DA_KV_HEADS * DA_DV, 2 * D_MODEL)
DN_ALPHA = (2 * DEPTH) ** 0.25
LN_EPS = 1e-5
RMS_EPS = 1e-6

LANES = 128
NEG_BIG = -1e30
VMEM_LIMIT = 56 * 1024 * 1024
PAGES_PER_STEP = 16
FLASH_BQ = 1024
FLASH_BK = 1024
ONES_ROWS = 16
Q_SCALE = math.log2(math.e) * DA_DH ** -0.5

NT_DIMS = (((1,), (1,)), ((), ()))
TN_DIMS = (((0,), (0,)), ((), ()))


def _params(sem):
    return pltpu.CompilerParams(dimension_semantics=sem, vmem_limit_bytes=VMEM_LIMIT)


def _dot(a, b):
    return jnp.dot(a, b, preferred_element_type=F32)


def _dot_nt(a, b):
    return lax.dot_general(a, b, NT_DIMS, preferred_element_type=F32)


def _dot_tn(a, b):
    return lax.dot_general(a, b, TN_DIMS, preferred_element_type=F32)


def _split3(a):
    hi = a.astype(BF16)
    r = a - hi.astype(F32)
    mid = r.astype(BF16)
    lo = (r - mid.astype(F32)).astype(BF16)
    return hi, mid, lo


def _sigmoid(x):
    return 1.0 / (1.0 + jnp.exp(-x))


def _layer_norm(v, g, b):
    mu = jnp.mean(v, axis=-1, keepdims=True)
    d = v - mu
    var = jnp.mean(d * d, axis=-1, keepdims=True)
    return d * lax.rsqrt(var + LN_EPS) * g + b


def _mm_kernel(x_ref, w_ref, o_ref):
    o_ref[...] = _dot(x_ref[...].astype(BF16), w_ref[...]).astype(o_ref.dtype)


def _mm_rope_kernel(x_ref, w_ref, c_ref, s1_ref, s2_ref, o_ref):
    acc = _dot(x_ref[...].astype(BF16), w_ref[...])
    c, s1, s2 = c_ref[...], s1_ref[...], s2_ref[...]
    for cb in range(acc.shape[1] // LANES):
        blk = acc[:, cb * LANES:(cb + 1) * LANES]
        up = pltpu.roll(blk, LANES - ROPE_DIM // 2, 1)
        dn = pltpu.roll(blk, ROPE_DIM // 2, 1)
        o_ref[:, cb * LANES:(cb + 1) * LANES] = (blk * c + up * s1 + dn * s2).astype(o_ref.dtype)


def _mm(x, w, out_dtype, bm, bn, rope=None):
    m, k = x.shape
    n = w.shape[1]
    bm, bn = min(bm, m), min(bn, n)
    if rope is not None:
        bm = min(bm, rope[0].shape[0])
    in_specs = [pl.BlockSpec((bm, k), lambda i, j: (i, 0)),
                pl.BlockSpec((k, bn), lambda i, j: (0, j))]
    args = [x, w]
    body = _mm_kernel
    if rope is not None:
        nrep = rope[0].shape[0] // bm
        for t in rope:
            in_specs.append(pl.BlockSpec((bm, LANES), lambda i, j: (i % nrep, 0)))
            args.append(t)
        body = _mm_rope_kernel
    return pl.pallas_call(
        body, grid=(m // bm, n // bn), in_specs=in_specs,
        out_specs=pl.BlockSpec((bm, bn), lambda i, j: (i, j)),
        out_shape=jax.ShapeDtypeStruct((m, n), out_dtype),
        compiler_params=_params(("parallel", "parallel")), name="proj",
    )(*args)


def _qkv_kernel(x_ref, w_ref, c_ref, s1_ref, s2_ref, qt_ref, k_ref, kb_ref, v_ref, vt_ref, *, q_scale):
    acc = _dot(x_ref[...].astype(BF16), w_ref[...])
    c, s1, s2 = c_ref[...], s1_ref[...], s2_ref[...]
    nq, nk = DA_HEADS * 2 * DA_DH // LANES, DA_KV_HEADS * 2 * DA_DH // LANES

    def rope(cb):
        blk = acc[:, cb * LANES:(cb + 1) * LANES]
        up = pltpu.roll(blk, LANES - ROPE_DIM // 2, 1)
        dn = pltpu.roll(blk, ROPE_DIM // 2, 1)
        return blk * c + up * s1 + dn * s2

    for cb in range(nq):
        qt_ref[0, cb * LANES:(cb + 1) * LANES, :] = (rope(cb) * q_scale).T.astype(BF16)
    for cb in range(nk):
        kblk = rope(nq + cb)
        k_ref[:, cb * LANES:(cb + 1) * LANES] = kblk
        kb_ref[:, cb * LANES:(cb + 1) * LANES] = kblk.astype(BF16)
    ones = jnp.ones((ONES_ROWS, acc.shape[0]), BF16)
    for h in range(DA_KV_HEADS):
        vblk = acc[:, (nq + nk + h) * LANES:(nq + nk + h + 1) * LANES]
        v_ref[:, h * DA_DV:(h + 1) * DA_DV] = vblk
        vt_ref[0, h, :DA_DV, :] = vblk.T.astype(BF16)
        vt_ref[0, h, DA_DV:, :] = ones


def _qkv(x, w_qkv, rope, nb, bm, q_scale):
    m, kdim = x.shape
    seq = m // nb
    bm = min(bm, seq)
    per_seq = seq // bm
    qw, kw = DA_HEADS * 2 * DA_DH, DA_KV_HEADS * 2 * DA_DH
    vw = DA_KV_HEADS * DA_DV
    row = lambda w: pl.BlockSpec((bm, w), lambda i: (i, 0))
    tab = pl.BlockSpec((bm, LANES), lambda i: (i % per_seq, 0))
    return pl.pallas_call(
        functools.partial(_qkv_kernel, q_scale=q_scale), grid=(m // bm,),
        in_specs=[row(kdim), pl.BlockSpec(w_qkv.shape, lambda i: (0, 0)), tab, tab, tab],
        out_specs=[pl.BlockSpec((1, qw, bm), lambda i: (i // per_seq, 0, i % per_seq)),
                   row(kw), row(kw), row(vw),
                   pl.BlockSpec((1, DA_KV_HEADS, DA_DV + ONES_ROWS, bm),
                                lambda i: (i // per_seq, 0, 0, i % per_seq))],
        out_shape=[jax.ShapeDtypeStruct((nb, qw, seq), BF16),
                   jax.ShapeDtypeStruct((m, kw), F32), jax.ShapeDtypeStruct((m, kw), BF16),
                   jax.ShapeDtypeStruct((m, vw), F32),
                   jax.ShapeDtypeStruct((nb, DA_KV_HEADS, DA_DV + ONES_ROWS, seq), BF16)],
        compiler_params=_params(("parallel",)), name="qkv_proj",
    )(x, w_qkv, *rope)


def _mm_res_ln_kernel(a_ref, w_ref, x_ref, g_ref, b_ref, o_ref):
    y = _dot(a_ref[...], w_ref[...])
    o_ref[...] = _layer_norm(DN_ALPHA * x_ref[...] + y, g_ref[...], b_ref[...])


def _mm_res_ln(a, w, x, g, b, bm):
    m, k = a.shape
    n = w.shape[1]
    bm = min(bm, m)
    return pl.pallas_call(
        _mm_res_ln_kernel, grid=(m // bm,),
        in_specs=[pl.BlockSpec((bm, k), lambda i: (i, 0)),
                  pl.BlockSpec((k, n), lambda i: (0, 0)),
                  pl.BlockSpec((bm, n), lambda i: (i, 0)),
                  pl.BlockSpec((1, n), lambda i: (0, 0)),
                  pl.BlockSpec((1, n), lambda i: (0, 0))],
        out_specs=pl.BlockSpec((bm, n), lambda i: (i, 0)),
        out_shape=jax.ShapeDtypeStruct((m, n), F32),
        compiler_params=_params(("parallel",)), name="proj_res_ln",
    )(a, w, x, g, b)


def _merge_kernel(y_ref, o_ref, g_ref, x_ref, wbr_ref, watt_ref, wo_ref, lg_ref, lb_ref, out_ref,
                  *, o_transposed):
    ssd = _dot(y_ref[...], wbr_ref[...])
    if o_transposed:
        att = _dot_tn(o_ref[0], watt_ref[...])
    else:
        att = _dot(o_ref[...], watt_ref[...])
    g = g_ref[...].astype(F32)
    merged = _sigmoid(g[:, :D_MODEL]) * ssd + _sigmoid(g[:, D_MODEL:]) * att
    mix = _dot(merged.astype(BF16), wo_ref[...])
    out_ref[...] = _layer_norm(DN_ALPHA * x_ref[...] + mix, lg_ref[...], lb_ref[...])


def _merge(y, o, g, x, wbr, watt, wo, lg, lb, bm):
    m = x.shape[0]
    bm = min(bm, m)
    row = lambda w: pl.BlockSpec((bm, w), lambda i: (i, 0))
    full = lambda a: pl.BlockSpec(a.shape, lambda i: (0, 0))
    o_transposed = o.ndim == 3
    if o_transposed:
        per_seq = o.shape[2] // bm
        o_spec = pl.BlockSpec((1, D_MODEL, bm), lambda i: (i // per_seq, 0, i % per_seq))
    else:
        o_spec = row(D_MODEL)
    return pl.pallas_call(
        functools.partial(_merge_kernel, o_transposed=o_transposed), grid=(m // bm,),
        in_specs=[row(SSD_INNER), o_spec, row(2 * D_MODEL), row(D_MODEL),
                  full(wbr), full(watt), full(wo), full(lg), full(lb)],
        out_specs=row(D_MODEL),
        out_shape=jax.ShapeDtypeStruct((m, D_MODEL), F32),
        compiler_params=_params(("parallel",)), name="merge",
    )(y, o, g, x, wbr, watt, wo, lg, lb)


def _mlp_kernel(x_ref, wu_ref, wd_ref, lg_ref, lb_ref, o_ref, acc_sc, xb_sc):
    j = pl.program_id(1)

    @pl.when(j == 0)
    def _():
        acc_sc[...] = jnp.zeros_like(acc_sc)
        xb_sc[...] = x_ref[...].astype(BF16)

    h = jnp.maximum(_dot(xb_sc[...], wu_ref[...]), 0.0)
    acc_sc[...] += _dot((h * h).astype(BF16), wd_ref[...])

    @pl.when(j == pl.num_programs(1) - 1)
    def _():
        o_ref[...] = _layer_norm(DN_ALPHA * x_ref[...] + acc_sc[...], lg_ref[...], lb_ref[...])


def _mlp(x, wu, wd, lg, lb, bm, bf):
    m = x.shape[0]
    bm = min(bm, m)
    return pl.pallas_call(
        _mlp_kernel, grid=(m // bm, D_FF // bf),
        in_specs=[pl.BlockSpec((bm, D_MODEL), lambda i, j: (i, 0)),
                  pl.BlockSpec((D_MODEL, bf), lambda i, j: (0, j)),
                  pl.BlockSpec((bf, D_MODEL), lambda i, j: (j, 0)),
                  pl.BlockSpec((1, D_MODEL), lambda i, j: (0, 0)),
                  pl.BlockSpec((1, D_MODEL), lambda i, j: (0, 0))],
        out_specs=pl.BlockSpec((bm, D_MODEL), lambda i, j: (i, 0)),
        out_shape=jax.ShapeDtypeStruct((m, D_MODEL), F32),
        scratch_shapes=[pltpu.VMEM((bm, D_MODEL), F32), pltpu.VMEM((bm, D_MODEL), BF16)],
        compiler_params=_params(("parallel", "arbitrary")), name="mlp",
    )(x, wu, wd, lg, lb)


def _cross_kernel(q_ref, k_ref, v_ref, o_ref):
    q = q_ref[0]
    k = k_ref[0]
    v = v_ref[0]
    for h in range(CA_HEADS):
        sl = slice(h * CA_DH, (h + 1) * CA_DH)
        s = _dot_nt(q[:, sl], k[:, sl]) * (CA_DH ** -0.5)
        s = s - jnp.max(s, axis=-1, keepdims=True)
        p = jnp.exp(s)
        p = p / jnp.sum(p, axis=-1, keepdims=True)
        o_ref[0, :, sl] = _dot(p.astype(BF16), v[:, sl]).astype(o_ref.dtype)


def _cross_block_kernel(x_ref, wq_ref, k_ref, v_ref, wo_ref, lg_ref, lb_ref, o_ref):
    x = x_ref[...]
    q = (_dot(x.astype(BF16), wq_ref[...]) * (CA_DH ** -0.5)).astype(BF16)
    k = k_ref[0]
    v = v_ref[0]
    heads = []
    for h in range(CA_HEADS):
        sl = slice(h * CA_DH, (h + 1) * CA_DH)
        s = _dot_nt(q[:, sl], k[:, sl])
        p = jnp.exp(s - jnp.max(s, axis=-1, keepdims=True))
        inv = 1.0 / jnp.sum(p, axis=-1, keepdims=True)
        heads.append((_dot(p.astype(BF16), v[:, sl]) * inv).astype(BF16))
    ca = _dot(jnp.concatenate(heads, axis=1), wo_ref[...])
    o_ref[...] = _layer_norm(DN_ALPHA * x + ca, lg_ref[...], lb_ref[...])


def _cross_block(x, wq, mk, mv, wo, lg, lb, bm):
    m, d = x.shape
    nb = mk.shape[0]
    per_seq = m // nb // bm
    row = pl.BlockSpec((bm, d), lambda i: (i, 0))
    full = lambda a: pl.BlockSpec(a.shape, lambda i: (0, 0))
    mem = pl.BlockSpec((1, N_MEM, d), lambda i: (i // per_seq, 0, 0))
    return pl.pallas_call(
        _cross_block_kernel, grid=(m // bm,),
        in_specs=[row, full(wq), mem, mem, full(wo), full(lg), full(lb)],
        out_specs=row, out_shape=jax.ShapeDtypeStruct((m, d), F32),
        compiler_params=_params(("parallel",)), name="cross_block",
    )(x, wq, mk, mv, wo, lg, lb)


def _cross(q, mk, mv, tq):
    b, t, d = q.shape
    tq = min(tq, t)
    return pl.pallas_call(
        _cross_kernel, grid=(b, t // tq),
        in_specs=[pl.BlockSpec((1, tq, d), lambda i, j: (i, j, 0)),
                  pl.BlockSpec((1, N_MEM, d), lambda i, j: (i, 0, 0)),
                  pl.BlockSpec((1, N_MEM, d), lambda i, j: (i, 0, 0))],
        out_specs=pl.BlockSpec((1, tq, d), lambda i, j: (i, j, 0)),
        out_shape=jax.ShapeDtypeStruct((b, t, d), BF16),
        compiler_params=_params(("parallel", "parallel")), name="cross_attn",
    )(q, mk, mv)


def _ssd_kernel(xbc_ref, dt_ref, z_ref, h0_ref, cprev_ref, cw_ref, cb_ref, dtb_ref, alog_ref,
                dskip_ref, nw_ref, tri_ref, eye_ref, exp_ref, y_ref, hout_ref,
                h_sc, xpad_sc, y_sc, *, q_len, valid):
    c = pl.program_id(1)

    @pl.when(c == 0)
    def _():
        h_sc[...] = h0_ref[0]
        xpad_sc[0:8, :] = cprev_ref[0]

    xpad_sc[8:8 + q_len, :] = xbc_ref[0].astype(F32)
    cw = cw_ref[...]
    conv = cb_ref[...] + cw[3:4, :] * xpad_sc[8:8 + q_len, :]
    for kk in range(1, CONV_W):
        conv = conv + cw[3 - kk:4 - kk, :] * xpad_sc[8 - kk:8 - kk + q_len, :]
    xpad_sc[0:8, :] = xpad_sc[q_len:q_len + 8, :]
    xbc = conv * _sigmoid(conv)
    xs = xbc[:, :SSD_INNER]

    dtr = dt_ref[0] + dtb_ref[...]
    dtv = jnp.maximum(dtr, 0.0) + jnp.log(1.0 + jnp.exp(-jnp.abs(dtr)))
    if valid < q_len:
        rows = lax.broadcasted_iota(jnp.int32, dtv.shape, 0)
        dtv = jnp.where(rows < valid, dtv, 0.0)
    a = dtv * (-jnp.exp(alog_ref[...]) * math.log2(math.e))

    tri = tri_ref[...]
    eye = eye_ref[...]
    expand = exp_ref[...]
    a_parts = _split3(a)
    acum = sum(_dot(tri, p) for p in a_parts)
    a_t = sum(_dot_nt(eye, p) for p in a_parts)
    acum_t = sum(_dot_nt(p, tri) for p in _split3(a_t))
    acum_x = sum(_dot(p, expand) for p in _split3(acum))
    dt_x = sum(_dot(p, expand) for p in _split3(dtv))

    xdt = xs * dt_x
    xdt_b = xdt.astype(BF16)
    xdec = (xdt * jnp.exp2(acum_x[q_len - 1:q_len, :] - acum_x)).astype(BF16)
    eacum_x = jnp.exp2(acum_x)
    chunk_decay = jnp.exp2(jnp.broadcast_to(acum_t[:, q_len - 1:q_len], (LANES, LANES)))

    li = lax.broadcasted_iota(jnp.int32, (q_len, q_len), 0)
    si = lax.broadcasted_iota(jnp.int32, (q_len, q_len), 1)
    causal = li >= si
    left = lax.broadcasted_iota(jnp.int32, (q_len, LANES), 1) < SSD_HEADDIM
    gw = SSD_HPG * SSD_HEADDIM
    for g in range(SSD_GROUPS):
        b_g = xbc[:, SSD_INNER + g * D_STATE:SSD_INNER + (g + 1) * D_STATE].astype(BF16)
        c_off = SSD_INNER + SSD_GROUPS * D_STATE
        c_g = xbc[:, c_off + g * D_STATE:c_off + (g + 1) * D_STATE].astype(BF16)
        cbm = _dot_nt(c_g, b_g)
        h_g = h_sc[g * gw:(g + 1) * gw, :]
        y_off = _dot_nt(c_g, h_g.astype(BF16))
        for pp in range(SSD_HPG // 2):
            j0 = g * SSD_HPG + 2 * pp
            cols = slice(j0 * SSD_HEADDIM, (j0 + 2) * SSD_HEADDIM)
            x_pair = xdt_b[:, cols]
            halves = []
            for j in (j0, j0 + 1):
                seg = acum[:, j:j + 1] - acum_t[j:j + 1, :]
                m_j = (cbm * jnp.where(causal, jnp.exp2(seg), 0.0)).astype(BF16)
                halves.append(_dot(m_j, x_pair))
            y_diag = jnp.where(left, halves[0], halves[1])
            y_sc[:, cols] = (y_diag + y_off[:, pp * LANES:(pp + 1) * LANES] * eacum_x[:, cols]
                             + xs[:, cols] * dskip_ref[:, cols])
        new_states = _dot_tn(xdec[:, g * gw:(g + 1) * gw], b_g)
        for jj in range(SSD_HPG):
            j = g * SSD_HPG + jj
            r = slice(j * SSD_HEADDIM, (j + 1) * SSD_HEADDIM)
            rl = slice(jj * SSD_HEADDIM, (jj + 1) * SSD_HEADDIM)
            h_sc[r, :] = h_g[rl, :] * chunk_decay[j:j + 1, :] + new_states[rl, :]

    z = z_ref[0].astype(F32)
    yz = y_sc[...] * (z * _sigmoid(z))
    ms = jnp.mean(yz * yz, axis=-1, keepdims=True)
    y_ref[0] = (yz * lax.rsqrt(ms + RMS_EPS) * nw_ref[...]).astype(y_ref.dtype)

    @pl.when(c == pl.num_programs(1) - 1)
    def _():
        hout_ref[0] = h_sc[...]


def _ssd(xbc, dtr, z, h0, cprev, cw, cb, dtb, alog, dskip_x, nw, consts, valid):
    b, l, _ = xbc.shape
    tri, eye, expand = consts
    q_len = tri.shape[0]
    nc = l // q_len
    tok = lambda w: pl.BlockSpec((1, q_len, w), lambda i, c: (i, c, 0))
    per_b = lambda r, w: pl.BlockSpec((1, r, w), lambda i, c: (i, 0, 0))
    full = lambda a: pl.BlockSpec(a.shape, lambda i, c: (0,) * a.ndim)
    return pl.pallas_call(
        functools.partial(_ssd_kernel, q_len=q_len, valid=valid),
        grid=(b, nc),
        in_specs=[tok(CONV_DIM), tok(LANES), tok(SSD_INNER),
                  per_b(SSD_INNER, D_STATE), per_b(8, CONV_DIM),
                  full(cw), full(cb), full(dtb), full(alog), full(dskip_x), full(nw),
                  full(tri), full(eye), full(expand)],
        out_specs=[tok(SSD_INNER), per_b(SSD_INNER, D_STATE)],
        out_shape=[jax.ShapeDtypeStruct((b, l, SSD_INNER), BF16),
                   jax.ShapeDtypeStruct((b, SSD_INNER, D_STATE), F32)],
        scratch_shapes=[pltpu.VMEM((SSD_INNER, D_STATE), F32),
                        pltpu.VMEM((q_len + 8, CONV_DIM), F32),
                        pltpu.VMEM((q_len, SSD_INNER), F32)],
        compiler_params=_params(("parallel", "arbitrary")), name="ssd",
    )(xbc, dtr, z, h0, cprev, cw, cb, dtb, alog, dskip_x, nw, tri, eye, expand)


def _flash_kernel(qm_ref, km_ref, lam_ref, qt_ref, k_ref, vt_ref, sw_ref, o_ref,
                  qw_sc, m_sc, acc_sc, *, bq, bk):
    step = pl.program_id(2)
    qi = qm_ref[step]
    kj = km_ref[step]
    cols = DA_GROUP * bq

    @pl.when(kj == 0)
    def _():
        row = lax.broadcasted_iota(jnp.int32, (LANES, bq), 0)
        for g in range(DA_GROUP):
            qg = qt_ref[0, 0, g].astype(F32)
            qw_sc[0, :, g * bq:(g + 1) * bq] = jnp.where(row < DA_DH, qg, 0.0).astype(BF16)
            qw_sc[1, :, g * bq:(g + 1) * bq] = jnp.where(row >= DA_DH, qg, 0.0).astype(BF16)
        m_sc[...] = jnp.full(m_sc.shape, NEG_BIG, F32)
        acc_sc[...] = jnp.zeros_like(acc_sc)

    def update(krows, cslices, masked):
        k = k_ref[0, krows, :]
        vt = vt_ref[0, 0, :, krows]
        nk = krows.stop - krows.start
        jobs = [(i, cs) for i in range(2) for cs in cslices]
        scores = [_dot(k, qw_sc[i, :, cs]) for i, cs in jobs]
        for (i, cs), st in zip(jobs, scores):
            if masked:
                w = cs.stop - cs.start
                key_minus_query = (kj * bk + krows.start - qi * bq - cs.start % bq
                                   + lax.broadcasted_iota(jnp.int32, (nk, w), 0)
                                   - lax.broadcasted_iota(jnp.int32, (nk, w), 1))
                st = jnp.where(key_minus_query <= 0, st, NEG_BIG)
            m_prev = m_sc[i, :, cs]
            m_new = jnp.maximum(m_prev, jnp.max(st, axis=0, keepdims=True))
            p = jnp.exp2(st - m_new).astype(BF16)
            acc_sc[i, :, cs] = jnp.exp2(m_prev - m_new) * acc_sc[i, :, cs] + _dot(vt, p)
            m_sc[i, :, cs] = m_new

    groups = [slice(g * bq, (g + 1) * bq) for g in range(DA_GROUP)]
    on_diag = kj * bk + bk - 1 > qi * bq

    @pl.when(on_diag)
    def _():
        if bq == bk:
            update(slice(0, bk // 2), groups, True)
            update(slice(bk // 2, bk), [slice(g.start + bq // 2, g.stop) for g in groups], True)
        else:
            update(slice(0, bk), groups, True)

    pl.when(jnp.logical_not(on_diag))(functools.partial(update, slice(0, bk), [slice(0, cols)], False))

    @pl.when(kj == (qi + 1) * (bq // bk) - 1)
    def _():
        lam = lam_ref[0]
        post = lam_ref[1]
        sw = jnp.concatenate([sw_ref[...]] * (bq // LANES), axis=1)
        for g in range(DA_GROUP):
            cs = slice(g * bq, (g + 1) * bq)
            o = (acc_sc[0, :DA_DV, cs] / acc_sc[0, DA_DV:DA_DV + 1, cs]
                 - lam * (acc_sc[1, :DA_DV, cs] / acc_sc[1, DA_DV:DA_DV + 1, cs]))
            ms = jnp.mean(o * o, axis=0, keepdims=True)
            o_ref[0, 0, g] = (o * lax.rsqrt(ms + RMS_EPS) * sw * post).astype(o_ref.dtype)


def _flash(lam2, qt, k, vt, sw_b, bq, bk):
    b, s, _ = k.shape
    dva = vt.shape[2]
    bq, bk = min(bq, s), min(bk, s)
    assert bq % bk == 0 and s % bq == 0 and bq % LANES == 0
    ratio = bq // bk
    pairs = [(qi, kj) for qi in range(s // bq) for kj in range((qi + 1) * ratio)]
    qmap = jnp.asarray([p[0] for p in pairs], jnp.int32)
    kmap = jnp.asarray([p[1] for p in pairs], jnp.int32)
    cols = DA_GROUP * bq
    grid_spec = pltpu.PrefetchScalarGridSpec(
        num_scalar_prefetch=2, grid=(b, DA_KV_HEADS, len(pairs)),
        in_specs=[pl.BlockSpec(memory_space=pltpu.SMEM),
                  pl.BlockSpec((1, 1, DA_GROUP, LANES, bq), lambda bi, h, st, qm, km: (bi, h, 0, 0, qm[st])),
                  pl.BlockSpec((1, bk, LANES), lambda bi, h, st, qm, km: (bi, km[st], h)),
                  pl.BlockSpec((1, 1, dva, bk), lambda bi, h, st, qm, km: (bi, h, 0, km[st])),
                  pl.BlockSpec((DA_DV, LANES), lambda bi, h, st, qm, km: (0, 0))],
        out_specs=pl.BlockSpec((1, 1, DA_GROUP, DA_DV, bq), lambda bi, h, st, qm, km: (bi, h, 0, 0, qm[st])),
        scratch_shapes=[pltpu.VMEM((2, LANES, cols), BF16),
                        pltpu.VMEM((2, 1, cols), F32),
                        pltpu.VMEM((2, dva, cols), F32)])
    return pl.pallas_call(
        functools.partial(_flash_kernel, bq=bq, bk=bk), grid_spec=grid_spec,
        out_shape=jax.ShapeDtypeStruct((b, DA_KV_HEADS, DA_GROUP, DA_DV, s), BF16),
        compiler_params=_params(("parallel", "parallel", "arbitrary")), name="flash_diff_attn",
    )(qmap, kmap, lam2, qt, k, vt, sw_b)


def _paged_kernel(pt_ref, lay_ref, lam_ref, q_ref, *rest, n_pages, npg):
    kp, vp = rest[:npg], rest[npg:2 * npg]
    knew_ref, vnew_ref, nmask_ref, sw_ref, o_ref, m_sc, l_sc, acc_sc = rest[2 * npg:]
    step = pl.program_id(1)
    n_steps = n_pages // npg
    hrows = q_ref.shape[1] // DA_KV_HEADS

    @pl.when(step == 0)
    def _():
        m_sc[...] = jnp.full(m_sc.shape, NEG_BIG, F32)
        l_sc[...] = jnp.zeros_like(l_sc)
        acc_sc[...] = jnp.zeros_like(acc_sc)

    def update(s, v_of_head):
        m_prev = m_sc[...]
        m_new = jnp.maximum(m_prev, jnp.max(s, axis=-1, keepdims=True))
        alpha = jnp.exp2(m_prev - m_new)
        p = jnp.exp2(s - jnp.concatenate([m_new] * (s.shape[1] // LANES), axis=1))
        l_sc[...] = alpha * l_sc[...] + jnp.sum(p, axis=-1, keepdims=True)
        pb = p.astype(BF16)
        for h in range(DA_KV_HEADS):
            rs = slice(h * hrows, (h + 1) * hrows)
            acc_sc[rs, :] = alpha[rs, :] * acc_sc[rs, :] + _dot(pb[rs, :], v_of_head(h))
        m_sc[...] = m_new

    @pl.when(step < n_steps)
    def _():
        kt = jnp.concatenate([r[0, 0].astype(BF16) for r in kp], axis=1)
        s = _dot(q_ref[0], kt)

        def v_of_head(h):
            return jnp.concatenate(
                [r[0, 0, pl.ds(h, PAGE_SIZE, stride=DA_KV_HEADS), :] for r in vp], axis=0).astype(BF16)

        update(s, v_of_head)

    @pl.when(step == n_steps)
    def _():
        s = _dot(q_ref[0], knew_ref[0].astype(BF16)) + nmask_ref[...]
        update(s, lambda h: vnew_ref[0][:, h * DA_DV:(h + 1) * DA_DV].astype(BF16))
        lam = lam_ref[0]
        post = lam_ref[1]
        half = hrows // 2
        for h in range(DA_KV_HEADS):
            r0 = slice(h * hrows, h * hrows + half)
            r1 = slice(h * hrows + half, (h + 1) * hrows)
            o = acc_sc[r0, :] / l_sc[r0, :] - lam * (acc_sc[r1, :] / l_sc[r1, :])
            ms = jnp.mean(o * o, axis=-1, keepdims=True)
            o_ref[0, h * half:(h + 1) * half, :] = o * lax.rsqrt(ms + RMS_EPS) * sw_ref[...] * post


def _paged(page_table, layer, lam2, qmat, cache_kt, cache_v2, knew_t, vnew, nmask, sw):
    nb, n_pages = page_table.shape
    npg = min(PAGES_PER_STEP, n_pages)
    assert n_pages % npg == 0
    n_steps = n_pages // npg
    nrows = qmat.shape[1]
    kvw = cache_kt.shape[2]

    def page_spec(o):
        def imap(b, s, pt, lay):
            return (lay[0], pt[b, jnp.minimum(s, n_steps - 1) * npg + o], 0, 0)
        return pl.BlockSpec((1, 1, kvw, LANES), imap)

    per_b = lambda a: pl.BlockSpec((1,) + a.shape[1:], lambda b, s, pt, lay: (b, 0, 0))
    full = lambda a: pl.BlockSpec(a.shape, lambda b, s, pt, lay: (0, 0))
    in_specs = ([pl.BlockSpec(memory_space=pltpu.SMEM), per_b(qmat)]
                + [page_spec(o) for o in range(npg)] + [page_spec(o) for o in range(npg)]
                + [per_b(knew_t), per_b(vnew), full(nmask), full(sw)])
    grid_spec = pltpu.PrefetchScalarGridSpec(
        num_scalar_prefetch=2, grid=(nb, n_steps + 1), in_specs=in_specs,
        out_specs=pl.BlockSpec((1, nrows // 2, DA_DV), lambda b, s, pt, lay: (b, 0, 0)),
        scratch_shapes=[pltpu.VMEM((nrows, LANES), F32), pltpu.VMEM((nrows, LANES), F32),
                        pltpu.VMEM((nrows, DA_DV), F32)])
    return pl.pallas_call(
        functools.partial(_paged_kernel, n_pages=n_pages, npg=npg), grid_spec=grid_spec,
        out_shape=jax.ShapeDtypeStruct((nb, nrows // 2, DA_DV), F32),
        compiler_params=_params(("parallel", "arbitrary")), name="paged_diff_attn",
    )(page_table, layer, lam2, qmat, *([cache_kt] * npg), *([cache_v2] * npg), knew_t, vnew, nmask, sw)


def _rope_tables(pos):
    half = ROPE_DIM // 2
    inv_freq = ROPE_THETA ** (-jnp.arange(half, dtype=F32) / half)
    ang = pos.astype(F32)[:, None] * inv_freq[None, :]
    cos, sin = jnp.cos(ang), jnp.sin(ang)
    t = pos.shape[0]
    pad = jnp.zeros((t, DA_DH - ROPE_DIM), F32)
    c = jnp.concatenate([cos, cos, jnp.ones_like(pad)], axis=1)
    s1 = jnp.concatenate([-sin, jnp.zeros_like(sin), pad], axis=1)
    s2 = jnp.concatenate([jnp.zeros_like(sin), sin, pad], axis=1)
    rep = LANES // DA_DH
    return tuple(jnp.tile(a, (1, rep)) for a in (c, s1, s2))


def _ssd_consts(q):
    tri = jnp.asarray(np.tril(np.ones((q, q), np.float32)), BF16)
    eye = jnp.asarray(np.eye(LANES, dtype=np.float32), BF16)
    expand = np.zeros((LANES, SSD_INNER), np.float32)
    for j in range(SSD_HEADS):
        expand[j, j * SSD_HEADDIM:(j + 1) * SSD_HEADDIM] = 1.0
    return tri, eye, jnp.asarray(expand, BF16)


def _pad_cols(a, width):
    return jnp.pad(a, [(0, 0)] * (a.ndim - 1) + [(0, width - a.shape[-1])])


def _ssd_gate_inputs(x2, wl, bm):
    z = _mm(x2, wl['w_z'], BF16, bm, 2048)
    xbc = _mm(x2, wl['w_xbc'], BF16, bm, 1536)
    dtr = _mm(x2, wl['w_dt'], F32, bm, LANES)
    g = _mm(x2, wl['w_g'], BF16, bm, 2048)
    return z, xbc, dtr, g


def _sample_qkv(x2, wl, rope, bm):
    q = _mm(x2, wl['w_q'], BF16, bm, 1024, tuple(t * Q_SCALE for t in rope))
    k = _mm(x2, wl['w_k'], F32, bm, 512, rope)
    v = _mm(x2, wl['w_v'], F32, bm, 512)
    return q, k, v


def _post_mixer(x2, y, o, g, mk, mv, wl, bm, tq):
    nb = mk.shape[0]
    x1 = _merge(y, o, g, x2, wl['w_ssd_br'], wl['w_att_br'], wl['w_o'], wl['ln1_g'], wl['ln1_b'], bm)
    t = x1.shape[0] // nb
    if t >= bm:
        x2n = _cross_block(x1, wl['w_cq'], mk, mv, wl['w_co'], wl['ln2_g'], wl['ln2_b'], bm)
    else:
        cq = _mm(x1, wl['w_cq'], BF16, bm, 1024)
        cq3 = jnp.pad(cq.reshape(nb, t, D_MODEL), ((0, 0), (0, tq - t), (0, 0)))
        co = _cross(cq3, mk, mv, tq)[:, :t].reshape(nb * t, D_MODEL)
        x2n = _mm_res_ln(co, wl['w_co'], x1, wl['ln2_g'], wl['ln2_b'], bm)
    return _mlp(x2n, wl['w_up'], wl['w_down'], wl['ln3_g'], wl['ln3_b'], bm, 2048)


def kernel(x_prompt, x_sample, mem_prompt, cache_k, cache_v, state_ssm, state_conv, cache_mem_k, cache_mem_v, page_table, w_in, conv_w, conv_b, dt_bias, a_log, d_skip, ssd_norm_w, w_ssd_br, lambda_q, lambda_k, subln_w, w_att_br, w_o, ln1_g, ln1_b, w_cq, w_ck, w_cv, w_co, ln2_g, ln2_b, w_up, w_down, ln3_g, ln3_b):
    nbp, seq, _ = x_prompt.shape
    nbs, tdec, _ = x_sample.shape
    n_pool = cache_k.shape[1]
    kvw = DA_KV_HEADS * DA_DV

    sp = tuple(int(v) for v in np.cumsum(IN_SPLITS))
    w_in_b = w_in.astype(BF16)
    rope_p = _rope_tables(jnp.arange(seq, dtype=jnp.int32))
    rope_s = _rope_tables(jnp.tile(PAST_LEN + jnp.arange(tdec, dtype=jnp.int32), nbs))
    consts_p = _ssd_consts(SSD_CHUNK)
    consts_s = _ssd_consts(SAMPLE_CHUNK)

    cache_kt = jnp.transpose(cache_k, (0, 1, 3, 4, 5, 2)).reshape(DEPTH, n_pool, kvw, PAGE_SIZE)
    cache_v2 = cache_v.reshape(DEPTH, n_pool, PAGE_SIZE * DA_KV_HEADS, DA_DV)

    nrow = DA_KV_HEADS * 2 * tdec * DA_GROUP
    row_t = (np.arange(nrow) % (tdec * DA_GROUP)) // DA_GROUP
    nmask = jnp.asarray(np.where(np.arange(LANES)[None, :] <= row_t[:, None], 0.0, NEG_BIG), F32)

    xp = x_prompt.reshape(nbp * seq, D_MODEL)
    xs = x_sample.reshape(nbs * tdec, D_MODEL)
    mem2 = mem_prompt.reshape(nbp * N_MEM, D_MODEL)
    zeros_h = jnp.zeros((nbp, SSD_INNER, D_STATE), F32)
    zeros_c = jnp.zeros((nbp, 8, CONV_DIM), F32)

    outs = [[] for _ in range(10)]
    for l in range(DEPTH):
        wl = {
            'w_z': w_in_b[l, :, :sp[0]], 'w_xbc': w_in_b[l, :, sp[0]:sp[1]],
            'w_dt': _pad_cols(w_in_b[l, :, sp[1]:sp[2]], LANES),
            'w_q': w_in_b[l, :, sp[2]:sp[3]], 'w_k': w_in_b[l, :, sp[3]:sp[4]],
            'w_v': w_in_b[l, :, sp[4]:sp[5]], 'w_g': w_in_b[l, :, sp[5]:sp[6]],
            'w_ssd_br': w_ssd_br[l].astype(BF16), 'w_att_br': w_att_br[l].astype(BF16),
            'w_o': w_o[l].astype(BF16), 'w_cq': w_cq[l].astype(BF16), 'w_co': w_co[l].astype(BF16),
            'w_up': w_up[l].astype(BF16), 'w_down': w_down[l].astype(BF16),
            'ln1_g': ln1_g[l][None], 'ln1_b': ln1_b[l][None], 'ln2_g': ln2_g[l][None],
            'ln2_b': ln2_b[l][None], 'ln3_g': ln3_g[l][None], 'ln3_b': ln3_b[l][None],
        }
        ssd_p = (conv_w[l], conv_b[l][None], _pad_cols(dt_bias[l][None], LANES),
                 _pad_cols(a_log[l][None], LANES), jnp.repeat(d_skip[l], SSD_HEADDIM)[None],
                 ssd_norm_w[l][None])
        lam_init = 0.8 - 0.6 * math.exp(-0.3 * l)
        lq = lambda_q[l].astype(F32)
        lk = lambda_k[l].astype(F32)
        lam = jnp.exp(jnp.sum(lq[0] * lk[0])) - jnp.exp(jnp.sum(lq[1] * lk[1])) + lam_init
        lam2 = jnp.stack([lam, jnp.asarray(1.0 - lam_init, F32)]).astype(F32)
        sw = subln_w[l][None]

        mk_p = _mm(mem2, w_ck[l].astype(BF16), F32, 512, 1024)
        mv_p = _mm(mem2, w_cv[l].astype(BF16), F32, 512, 1024)
        z, xbc, dtr, g = _ssd_gate_inputs(xp, wl, 1024)
        qt, k, kb, v, vt = _qkv(xp, w_in_b[l, :, sp[2]:sp[5]], rope_p, nbp, 512, Q_SCALE)
        y, h_p = _ssd(xbc.reshape(nbp, seq, CONV_DIM), dtr.reshape(nbp, seq, LANES),
                      z.reshape(nbp, seq, SSD_INNER), zeros_h, zeros_c, *ssd_p, consts_p, valid=SSD_CHUNK)
        o_t = _flash(lam2, qt.reshape(nbp, DA_KV_HEADS, DA_GROUP, LANES, seq), kb.reshape(nbp, seq, kvw), vt,
                     jnp.broadcast_to(subln_w[l][:, None], (DA_DV, LANES)), FLASH_BQ, FLASH_BK)
        xp = _post_mixer(xp, y.reshape(nbp * seq, SSD_INNER), o_t.reshape(nbp, D_MODEL, seq), g,
                         mk_p.astype(BF16).reshape(nbp, N_MEM, D_MODEL),
                         mv_p.astype(BF16).reshape(nbp, N_MEM, D_MODEL), wl, 512, 512)
        c_p = xbc.reshape(nbp, seq, CONV_DIM)[:, seq - (CONV_W - 1):].astype(F32)

        zs, xbcs, dtrs, gs = _ssd_gate_inputs(xs, wl, LANES)
        qs, ks, vs = _sample_qkv(xs, wl, rope_s, LANES)
        pad_t = lambda a: jnp.pad(a.reshape(nbs, tdec, -1), ((0, 0), (0, SAMPLE_CHUNK - tdec), (0, 0)))
        cprev = jnp.pad(state_conv[l], ((0, 0), (8 - (CONV_W - 1), 0), (0, 0)))
        ys, h_s = _ssd(pad_t(xbcs), pad_t(dtrs), pad_t(zs),
                       state_ssm[l].reshape(nbs, SSD_INNER, D_STATE), cprev, *ssd_p, consts_s, valid=tdec)
        ys = ys[:, :tdec].reshape(nbs * tdec, SSD_INNER)
        c_s = jnp.concatenate([state_conv[l], xbcs.reshape(nbs, tdec, CONV_DIM).astype(F32)],
                              axis=1)[:, -(CONV_W - 1):]

        q6 = qs.astype(F32).reshape(nbs, tdec, DA_KV_HEADS, DA_GROUP, 2, DA_DH)
        q8 = jnp.transpose(q6, (0, 2, 4, 1, 3, 5)).reshape(nbs, DA_KV_HEADS * 2, tdec * DA_GROUP, DA_DH)
        qmat = jnp.einsum('bktd,kj->bktjd', q8, jnp.eye(DA_KV_HEADS * 2, dtype=F32))
        qmat = qmat.reshape(nbs, nrow, kvw).astype(BF16)
        knew_t = _pad_cols(jnp.transpose(ks.reshape(nbs, tdec, kvw), (0, 2, 1)), LANES)
        vnew = jnp.pad(vs.reshape(nbs, tdec, kvw), ((0, 0), (0, LANES - tdec), (0, 0)))
        o_s = _paged(page_table, jnp.full((1,), l, jnp.int32), lam2, qmat,
                     cache_kt, cache_v2, knew_t, vnew, nmask, sw)
        o_s = jnp.transpose(o_s.reshape(nbs, DA_KV_HEADS, tdec, DA_GROUP, DA_DV), (0, 2, 1, 3, 4))
        o_s = o_s.reshape(nbs * tdec, D_MODEL).astype(BF16)
        xs = _post_mixer(xs, ys, o_s, gs, cache_mem_k[l].astype(BF16).reshape(nbs, N_MEM, D_MODEL),
                         cache_mem_v[l].astype(BF16).reshape(nbs, N_MEM, D_MODEL), wl, LANES, 16)

        vals = (k.reshape(nbp, seq, DA_KV_HEADS, 2, DA_DH), v.reshape(nbp, seq, DA_KV_HEADS, DA_DV),
                h_p.reshape(nbp, SSD_HEADS, SSD_HEADDIM, D_STATE), c_p,
                mk_p.reshape(nbp, N_MEM, CA_HEADS, CA_DH), mv_p.reshape(nbp, N_MEM, CA_HEADS, CA_DH),
                ks.reshape(nbs, tdec, DA_KV_HEADS, 2, DA_DH), vs.reshape(nbs, tdec, DA_KV_HEADS, DA_DV),
                h_s.reshape(nbs, SSD_HEADS, SSD_HEADDIM, D_STATE), c_s)
        for lst, val in zip(outs, vals):
            lst.append(val)

    return (xp.reshape(nbp, seq, D_MODEL), xs.reshape(nbs, tdec, D_MODEL)) + tuple(jnp.stack(o) for o in outs)
```
